```python
import math
import jax
import jax.numpy as jnp
from jax import lax
import numpy as np

D_MODEL = 2048
BATCH = 16
SEQ = 2048
DEPTH = 4

CTX_LEN = 256
GRID_W = 64
ROPE_BASE = 10000.0
BLOCK = 128

MIX_W = D_MODEL // 2
DA = 64
HA = MIX_W // (2 * DA)
DB = 128
HB = MIX_W // DB
HB_KV = 2
WINDOW = 128
DKC = 128
DVC = 128
HC = MIX_W // DVC
CONV_K = 5
CHUNK = 64
N_BRANCH = 3
N_EXPERTS = 64
TOP_K = 8
D_EXPERT = 256
D_SHARED = 256
ROUTED_SCALE = 2.5

DEEPNORM_ALPHA = (2 * DEPTH) ** 0.25
DEEPNORM_BETA = (8 * DEPTH) ** -0.25
LN_EPS = 1e-5
RMS_EPS = 1e-6

KV_SIZES = (HA * 2 * DA, HA * 2 * DA, HB_KV * DB, HB_KV * DB, HC * DKC, HC * DKC, HC * DVC, 2 * HC, 2 * HC)
Q_SIZES = (HA * 2 * DA, HB * DB, HC * DVC, N_BRANCH * D_MODEL)
N_CTX_COLS = sum(KV_SIZES)
N_IN = N_CTX_COLS + sum(Q_SIZES)

kernel_name = 'hybrid_diff_swa_deltanet_moe_deepnorm'

F32 = jnp.float32


def split_cols(p, sizes):
    idx = np.cumsum(sizes)[:-1].tolist()
    return jnp.split(p[..., :sum(sizes)], idx, axis=-1)


def layer_norm(x, g, b):
    xf = x.astype(F32)
    mu = jnp.mean(xf, axis=-1, keepdims=True)
    var = jnp.mean(jnp.square(xf - mu), axis=-1, keepdims=True)
    return ((xf - mu) * lax.rsqrt(var + LN_EPS) * g.astype(F32) + b.astype(F32)).astype(x.dtype)


def rms_norm(x, w):
    xf = x.astype(F32)
    return (xf * lax.rsqrt(jnp.mean(jnp.square(xf), axis=-1, keepdims=True) + RMS_EPS) * w.astype(F32)).astype(x.dtype)


def l2_normalize(x):
    xf = x.astype(F32)
    return (xf * lax.rsqrt(jnp.sum(jnp.square(xf), axis=-1, keepdims=True) + RMS_EPS)).astype(x.dtype)


def modulate(h, shift, scale):
    return h * (1.0 + scale) + shift


def axial_rope_tables(row, col, d):
    half = d // 2
    inv = jnp.power(ROPE_BASE, -jnp.arange(0, half, 2, dtype=F32) / half)
    ang_r = row.astype(F32)[:, None] * inv[None, :]
    ang_c = col.astype(F32)[:, None] * inv[None, :]
    ang = jnp.concatenate([ang_r, ang_r, ang_c, ang_c], axis=-1)
    return jnp.cos(ang), jnp.sin(ang)


def apply_rope(x, cos, sin):
    d = x.shape[-1]
    half, qt = d // 2, d // 4
    rot = jnp.concatenate([-x[..., qt:half], x[..., :qt], -x[..., half + qt:], x[..., half:half + qt]], axis=-1)
    return (x * cos[:, None, :] + rot * sin[:, None, :]).astype(x.dtype)


def short_conv(x, w):
    pad = CONV_K // 2
    return lax.conv_general_dilated(x, w[:, None, :], window_strides=(1,), padding=[(pad, pad)],
                                    dimension_numbers=('NWC', 'WIO', 'NWC'), feature_group_count=x.shape[-1])


def project_stream(p, conv_w, a_log, dt_bias, rope_a, rope_b):
    B, T = p.shape[:2]
    ak, av, bk, bv, cq, ck, cv, cb, ca = split_cols(p, KV_SIZES)
    ak = ak.reshape(B, T, 2 * HA, DA)
    bk = bk.reshape(B, T, HB_KV, DB)
    if rope_a is not None:
        ak = apply_rope(ak, *rope_a)
        bk = apply_rope(bk, *rope_b)
    av = av.reshape(B, T, HA, 2 * DA)
    bv = bv.reshape(B, T, HB_KV, DB)
    qkv = jax.nn.silu(short_conv(jnp.concatenate([cq, ck, cv], axis=-1), conv_w))
    cq, ck, cv = jnp.split(qkv, 3, axis=-1)
    cq = l2_normalize(cq.reshape(B, T, HC, DKC)) * (DKC ** -0.5)
    ck = l2_normalize(ck.reshape(B, T, HC, DKC))
    cv = cv.reshape(B, T, HC, DVC)
    beta = jax.nn.sigmoid(cb.astype(F32)).reshape(B, T, 2, HC)
    g = -jnp.exp(a_log.astype(F32)) * jax.nn.softplus(ca.astype(F32).reshape(B, T, 2, HC) + dt_bias.astype(F32))
    return ak, av, bk, bv, (cq, ck, cv, beta, g)


def diff_attn_block(q, k, v, lam):
    B, Tq = q.shape[:2]
    s = jnp.einsum('bqhd,bkhd->bhqk', q, k).astype(F32) * (DA ** -0.5)
    p = jax.nn.softmax(s, axis=-1).reshape(B, HA, 2, Tq, -1)
    p = p[:, :, 0] - lam * p[:, :, 1]
    return jnp.einsum('bhqk,bkhd->bqhd', p.astype(v.dtype), v)


def diff_attention(q, k, v, lam, lam_init, norm_w):
    B, Tq = q.shape[:2]
    nq = Tq // BLOCK
    qb = jnp.moveaxis(q.reshape(B, nq, BLOCK, 2 * HA, DA), 1, 0)
    o = lax.map(lambda qi: diff_attn_block(qi, k, v, lam), qb)
    o = jnp.moveaxis(o, 0, 1).reshape(B, Tq, HA, 2 * DA)
    o = rms_norm(o, norm_w) * (1.0 - lam_init)
    return o.reshape(B, Tq, MIX_W)


def band_blocks(t):
    B, L = t.shape[:2]
    nb = L // BLOCK
    tp = jnp.pad(t, ((0, 0), (BLOCK, BLOCK), (0, 0), (0, 0))).reshape((B, nb + 2, BLOCK) + t.shape[2:])
    return jnp.concatenate([tp[:, :-2], tp[:, 1:-1], tp[:, 2:]], axis=2)


def swa_latent(q, k, v, kc, vc, sink):
    B, L = q.shape[:2]
    nb = L // BLOCK
    G = HB // HB_KV
    C = kc.shape[1]
    qg = q.reshape(B, nb, BLOCK, HB_KV, G, DB)
    kb, vb = band_blocks(k), band_blocks(v)
    scale = DB ** -0.5
    s_loc = jnp.einsum('bnqhgd,bnshd->bnhgqs', qg, kb).astype(F32) * scale
    blk = jnp.arange(nb)[:, None]
    qpos = blk * BLOCK + jnp.arange(BLOCK)[None, :]
    kpos = (blk - 1) * BLOCK + jnp.arange(3 * BLOCK)[None, :]
    rel = kpos[:, None, :] - qpos[:, :, None]
    valid = (jnp.abs(rel) <= WINDOW) & (kpos[:, None, :] >= 0) & (kpos[:, None, :] < L)
    s_loc = jnp.where(valid[None, :, None, None], s_loc, -jnp.inf)
    s_ctx = jnp.einsum('bnqhgd,bchd->bnhgqc', qg, kc).astype(F32) * scale
    s_sink = jnp.broadcast_to(sink.astype(F32).reshape(1, 1, HB_KV, G, 1, 1), s_ctx.shape[:-1] + (1,))
    p = jax.nn.softmax(jnp.concatenate([s_loc, s_ctx, s_sink], axis=-1), axis=-1).astype(v.dtype)
    o = (jnp.einsum('bnhgqs,bnshd->bnqhgd', p[..., :3 * BLOCK], vb)
         + jnp.einsum('bnhgqc,bchd->bnqhgd', p[..., 3 * BLOCK:3 * BLOCK + C], vc))
    return o.reshape(B, L, MIX_W)


def swa_context(q, k, v, sink):
    B, C = q.shape[:2]
    G = HB // HB_KV
    qg = q.reshape(B, C, HB_KV, G, DB)
    s = jnp.einsum('bqhgd,bkhd->bhgqk', qg, k).astype(F32) * (DB ** -0.5)
    s_sink = jnp.broadcast_to(sink.astype(F32).reshape(1, HB_KV, G, 1, 1), s.shape[:-1] + (1,))
    p = jax.nn.softmax(jnp.concatenate([s, s_sink], axis=-1), axis=-1)[..., :C].astype(v.dtype)
    return jnp.einsum('bhgqk,bkhd->bqhgd', p, v).reshape(B, C, MIX_W)


def gated_delta_chunked(q, k, v, beta, g, s0, with_output):
    B, T, H, _ = q.shape
    DV = v.shape[-1]
    n = T // CHUNK
    out_dtype = v.dtype

    def blocks(t):
        t = t.astype(F32).reshape((B, n, CHUNK) + t.shape[2:])
        return t.transpose((1, 0, 3, 2) + tuple(range(4, t.ndim)))

    q, k, v, beta, g = blocks(q), blocks(k), blocks(v), blocks(beta), blocks(g)
    gc = jnp.cumsum(g, axis=-1)
    incl = jnp.tril(jnp.ones((CHUNK, CHUNK), dtype=bool))
    strict = jnp.tril(jnp.ones((CHUNK, CHUNK), dtype=bool), -1)
    decay = jnp.exp(jnp.where(incl, gc[..., :, None] - gc[..., None, :], -jnp.inf))
    kb = k * beta[..., None]
    a = jnp.where(strict, jnp.einsum('nbhcd,nbhsd->nbhcs', kb, k) * decay, 0.0)
    eye = jnp.eye(CHUNK, dtype=F32)
    t_inv = lax.linalg.triangular_solve(a + eye, jnp.broadcast_to(eye, a.shape), left_side=True,
                                        lower=True, unit_diagonal=True)
    u0 = jnp.einsum('nbhcs,nbhsd->nbhcd', t_inv, v * beta[..., None])
    w = jnp.einsum('nbhcs,nbhsd->nbhcd', t_inv, kb * jnp.exp(gc)[..., None])
    k_tail = k * jnp.exp(gc[..., -1:] - gc)[..., None]
    chunk_decay = jnp.exp(gc[..., -1])

    def new_values(S, u0_i, w_i):
        return u0_i - jnp.einsum('bhck,bhkv->bhcv', w_i, S)

    def next_state(S, u, kt_i, cd_i):
        return S * cd_i[..., None, None] + jnp.einsum('bhck,bhcv->bhkv', kt_i, u)

    if with_output:
        qk = jnp.einsum('nbhcd,nbhsd->nbhcs', q, k) * decay
        q_dec = q * jnp.exp(gc)[..., None]

        def step(S, xs):
            u0_i, w_i, kt_i, cd_i, qk_i, qd_i = xs
            u = new_values(S, u0_i, w_i)
            o = jnp.einsum('bhck,bhkv->bhcv', qd_i, S) + jnp.einsum('bhcs,bhsv->bhcv', qk_i, u)
            return next_state(S, u, kt_i, cd_i), o

        S, o = lax.scan(step, s0, (u0, w, k_tail, chunk_decay, qk, q_dec))
        o = o.transpose(1, 0, 3, 2, 4).reshape(B, T, H, DV).astype(out_dtype)
        return o, S

    def step_state(S, xs):
        u0_i, w_i, kt_i, cd_i = xs
        return next_state(S, new_values(S, u0_i, w_i), kt_i, cd_i), None

    S, _ = lax.scan(step_state, s0, (u0, w, k_tail, chunk_decay))
    return None, S


def flip_if(t, d):
    return jnp.flip(t, axis=1) if d == 1 else t


def delta_bidirectional(lat, ctx, with_ctx_out):
    ql, kl, vl, bl, gl = lat
    qc, kc, vc, bc, gcx = ctx
    s0 = jnp.zeros((ql.shape[0], HC, DKC, DVC), F32)
    lat_outs, ctx_outs = [], []
    for d in range(2):
        oc, s_ctx = gated_delta_chunked(flip_if(qc, d), flip_if(kc, d), flip_if(vc, d), flip_if(bc[:, :, d], d),
                                        flip_if(gcx[:, :, d], d), s0, with_ctx_out)
        ol, _ = gated_delta_chunked(flip_if(ql, d), flip_if(kl, d), flip_if(vl, d), flip_if(bl[:, :, d], d),
                                    flip_if(gl[:, :, d], d), s_ctx, True)
        lat_outs.append(flip_if(ol, d))
        if with_ctx_out:
            ctx_outs.append(flip_if(oc, d))
    o_ctx = ctx_outs[0] + ctx_outs[1] if with_ctx_out else None
    return lat_outs[0] + lat_outs[1], o_ctx


def gated_output(o, z, norm_w):
    B, T = o.shape[:2]
    return (rms_norm(o, norm_w) * jax.nn.silu(z.reshape(B, T, HC, DVC))).reshape(B, T, MIX_W)


def merge_branches(outs, gates, w_branch, w_out):
    g = jnp.split(jax.nn.sigmoid(gates.astype(F32)).astype(gates.dtype), N_BRANCH, axis=-1)
    merged = g[0] * (outs[0] @ w_branch[0])
    for i in range(1, N_BRANCH):
        merged = merged + g[i] * (outs[i] @ w_branch[i])
    return merged @ w_out


def swiglu(h, wg, wu, wd):
    return (jax.nn.silu(h @ wg) * (h @ wu)) @ wd


def moe_ffn(h, w_router, router_bias, w_gate_e, w_up_e, w_down_e, w_gate_s, w_up_s, w_down_s):
    scores = jax.nn.sigmoid((h @ w_router).astype(F32))
    _, idx = lax.top_k(scores + router_bias.astype(F32), TOP_K)
    sel = jnp.take_along_axis(scores, idx, axis=-1)
    wts = sel / jnp.sum(sel, axis=-1, keepdims=True) * ROUTED_SCALE
    gate = jnp.sum(jax.nn.one_hot(idx, N_EXPERTS, dtype=F32) * wts[..., None], axis=-2).astype(h.dtype)
    y = swiglu(h, w_gate_s, w_up_s, w_down_s)
    for e in range(N_EXPERTS):
        y = y + gate[..., e:e + 1] * swiglu(h, w_gate_e[e], w_up_e[e], w_down_e[e])
    return y


def setup_inputs(seed: int = 0) -> dict:
    key = jax.random.key(seed)
    ks = iter(jax.random.split(key, 40))
    D = D_MODEL

    def nrm(shape, s):
        return jax.random.normal(next(ks), shape, F32) * s

    dt = jnp.exp(jax.random.uniform(next(ks), (DEPTH, 2, HC), F32, math.log(1e-3), math.log(1e-1)))
    return {
        'x': nrm((BATCH, SEQ, D), 1.0),
        'c': nrm((BATCH, D), 1.0),
        'ctx': nrm((BATCH, CTX_LEN, D), 1.0),
        'c_ctx': nrm((D,), 1.0),
        'w_ada': nrm((DEPTH, D, 6 * D), 0.5 * D ** -0.5),
        'b_ada': nrm((DEPTH, 6 * D), 0.01),
        'w_in': nrm((DEPTH, D, N_IN), D ** -0.5),
        'lam_q1': nrm((DEPTH, DA), 0.1),
        'lam_k1': nrm((DEPTH, DA), 0.1),
        'lam_q2': nrm((DEPTH, DA), 0.1),
        'lam_k2': nrm((DEPTH, DA), 0.1),
        'diff_norm_w': 1.0 + nrm((DEPTH, 2 * DA), 0.02),
        'sink': nrm((DEPTH, HB), 0.5),
        'conv_w': nrm((DEPTH, CONV_K, 3 * MIX_W), CONV_K ** -0.5),
        'a_log': jnp.log(jax.random.uniform(next(ks), (DEPTH, 2, HC), F32, 1.0, 16.0)),
        'dt_bias': dt + jnp.log(-jnp.expm1(-dt)),
        'delta_norm_w': 1.0 + nrm((DEPTH, DVC), 0.02),
        'w_branch': nrm((DEPTH, N_BRANCH, MIX_W, D), MIX_W ** -0.5 * DEEPNORM_BETA),
        'w_out': nrm((DEPTH, D, D), D ** -0.5 * DEEPNORM_BETA),
        'ln1_g': 1.0 + nrm((DEPTH, D), 0.02),
        'ln1_b': nrm((DEPTH, D), 0.01),
        'ln2_g': 1.0 + nrm((DEPTH, D), 0.02),
        'ln2_b': nrm((DEPTH, D), 0.01),
        'w_router': nrm((DEPTH, D, N_EXPERTS), D ** -0.5),
        'router_bias': nrm((DEPTH, N_EXPERTS), 0.01),
        'w_gate_e': nrm((DEPTH, N_EXPERTS, D, D_EXPERT), D ** -0.5),
        'w_up_e': nrm((DEPTH, N_EXPERTS, D, D_EXPERT), D ** -0.5),
        'w_down_e': nrm((DEPTH, N_EXPERTS, D_EXPERT, D), D_EXPERT ** -0.5 * DEEPNORM_BETA),
        'w_gate_s': nrm((DEPTH, D, D_SHARED), D ** -0.5),
        'w_up_s': nrm((DEPTH, D, D_SHARED), D ** -0.5),
        'w_down_s': nrm((DEPTH, D_SHARED, D), D_SHARED ** -0.5 * DEEPNORM_BETA),
    }


def reference(x, c, ctx, c_ctx, w_ada, b_ada, w_in, lam_q1, lam_k1, lam_q2, lam_k2, diff_norm_w, sink,
              conv_w, a_log, dt_bias, delta_norm_w, w_branch, w_out, ln1_g, ln1_b, ln2_g, ln2_b,
              w_router, router_bias, w_gate_e, w_up_e, w_down_e, w_gate_s, w_up_s, w_down_s):
    B, L = x.shape[:2]
    C = ctx.shape[1]
    ROWS = L // GRID_W
    row = jnp.repeat(jnp.arange(ROWS), GRID_W)
    col = jnp.tile(jnp.arange(GRID_W), ROWS)
    rope_a = axial_rope_tables(row, col, DA)
    rope_b = axial_rope_tables(row, col, DB)
    silu_c = jax.nn.silu(c)
    silu_cc = jax.nn.silu(c_ctx)
    h_lat, h_ctx = x, ctx
    for l in range(DEPTH):
        last = l == DEPTH - 1
        mod_lat = (silu_c @ w_ada[l] + b_ada[l])[:, None, :]
        mod_ctx = (silu_cc @ w_ada[l] + b_ada[l])[None, None, :]
        sh1, sc1, g1, sh2, sc2, g2 = jnp.split(mod_lat, 6, axis=-1)
        csh1, csc1, cg1, csh2, csc2, cg2 = jnp.split(mod_ctx, 6, axis=-1)
        lam_init = 0.8 - 0.6 * math.exp(-0.3 * l)
        lam = (jnp.exp(jnp.sum(lam_q1[l].astype(F32) * lam_k1[l].astype(F32)))
               - jnp.exp(jnp.sum(lam_q2[l].astype(F32) * lam_k2[l].astype(F32))) + lam_init)

        p_lat = modulate(h_lat, sh1, sc1) @ w_in[l]
        w_in_ctx = w_in[l, :, :N_CTX_COLS] if last else w_in[l]
        p_ctx = modulate(h_ctx, csh1, csc1) @ w_in_ctx
        lat_ak, lat_av, lat_bk, lat_bv, lat_delta = project_stream(p_lat, conv_w[l], a_log[l], dt_bias[l], rope_a, rope_b)
        ctx_ak, ctx_av, ctx_bk, ctx_bv, ctx_delta = project_stream(p_ctx, conv_w[l], a_log[l], dt_bias[l], None, None)
        o_c_lat, o_c_ctx = delta_bidirectional(lat_delta, ctx_delta, not last)

        qa, qb, z, gates = split_cols(p_lat[..., N_CTX_COLS:], Q_SIZES)
        qa = apply_rope(qa.reshape(B, L, 2 * HA, DA), *rope_a)
        qb = apply_rope(qb.reshape(B, L, HB, DB), *rope_b)
        o_a = diff_attention(qa, jnp.concatenate([lat_ak, ctx_ak], axis=1), jnp.concatenate([lat_av, ctx_av], axis=1),
                             lam, lam_init, diff_norm_w[l])
        o_b = swa_latent(qb, lat_bk, lat_bv, ctx_bk, ctx_bv, sink[l])
        o_c = gated_output(o_c_lat, z, delta_norm_w[l])
        mix = merge_branches((o_a, o_b, o_c), gates, w_branch[l], w_out[l])
        h_lat = layer_norm(DEEPNORM_ALPHA * h_lat + g1 * mix, ln1_g[l], ln1_b[l])

        if last:
            f = moe_ffn(modulate(h_lat, sh2, sc2), w_router[l], router_bias[l], w_gate_e[l], w_up_e[l],
                        w_down_e[l], w_gate_s[l], w_up_s[l], w_down_s[l])
            h_lat = layer_norm(DEEPNORM_ALPHA * h_lat + g2 * f, ln2_g[l], ln2_b[l])
        else:
            cqa, cqb, cz, cgates = split_cols(p_ctx[..., N_CTX_COLS:], Q_SIZES)
            co_a = diff_attention(cqa.reshape(B, C, 2 * HA, DA), ctx_ak, ctx_av, lam, lam_init, diff_norm_w[l])
            co_b = swa_context(cqb.reshape(B, C, HB, DB), ctx_bk, ctx_bv, sink[l])
            co_c = gated_output(o_c_ctx, cz, delta_norm_w[l])
            cmix = merge_branches((co_a, co_b, co_c), cgates, w_branch[l], w_out[l])
            h_ctx = layer_norm(DEEPNORM_ALPHA * h_ctx + cg1 * cmix, ln1_g[l], ln1_b[l])

            f_in = jnp.concatenate([modulate(h_lat, sh2, sc2), modulate(h_ctx, csh2, csc2)], axis=1)
            f = moe_ffn(f_in, w_router[l], router_bias[l], w_gate_e[l], w_up_e[l], w_down_e[l],
                        w_gate_s[l], w_up_s[l], w_down_s[l])
            h_lat = layer_norm(DEEPNORM_ALPHA * h_lat + g2 * f[:, :L], ln2_g[l], ln2_b[l])
            h_ctx = layer_norm(DEEPNORM_ALPHA * h_ctx + cg2 * f[:, L:], ln2_g[l], ln2_b[l])
    return h_lat
```

```python
import functools
import math

import jax
import jax.numpy as jnp
from jax import lax
from jax.experimental import pallas as pl
from jax.experimental.pallas import tpu as pltpu

F32 = jnp.float32
BF16 = jnp.bfloat16
HIGHEST = lax.Precision.HIGHEST

LANE = 128
DA = 64
DB = 128
HB_KV = 2
DKC = 128
CONV_K = 5
CHUNK = 64
BLOCK = 128
WINDOW = 128
GRID_W = 64
ROPE_BASE = 10000.0
TOP_K = 8
ROUTED_SCALE = 2.5
LN_EPS = 1e-5
RMS_EPS = 1e-6
NEG = -1e30
VMEM_LIMIT = 56 * 1024 * 1024


def _cparams(sem):
    return pltpu.CompilerParams(dimension_semantics=sem, vmem_limit_bytes=VMEM_LIMIT)


def _dot(a, b, precision=None):
    return jnp.dot(a, b, preferred_element_type=F32, precision=precision)


def _dot_nt(a, b, precision=None):
    return lax.dot_general(a, b, (((1,), (1,)), ((), ())), preferred_element_type=F32, precision=precision)


def _dot_tn(a, b, precision=None):
    return lax.dot_general(a, b, (((0,), (0,)), ((), ())), preferred_element_type=F32, precision=precision)


def _silu(x):
    return x * jax.nn.sigmoid(x)


def _softplus(x):
    return jnp.maximum(x, 0.0) + jnp.log1p(jnp.exp(-jnp.abs(x)))


def _layer_norm(y, g, b):
    mu = jnp.mean(y, axis=-1, keepdims=True)
    yc = y - mu
    var = jnp.mean(yc * yc, axis=-1, keepdims=True)
    return yc * lax.rsqrt(var + LN_EPS) * g + b


def _pick(tile, S, L):
    assert S % tile == 0, (S, tile)
    return tile


def _is_ctx_rows(i, tm, S, L):
    row = (i * tm) % S + lax.broadcasted_iota(jnp.int32, (tm, 1), 0)
    return row >= L


def _mods_kernel(c_ref, w_ref, b_ref, o_ref):
    o_ref[0] = _dot(_silu(c_ref[...]), w_ref[0], HIGHEST) + b_ref[0]


def compute_mods(c_all, w_ada, b_ada):
    depth, D, n6 = w_ada.shape
    mb = c_all.shape[0]
    tn = 1024
    return pl.pallas_call(
        _mods_kernel,
        grid=(depth, n6 // tn),
        in_specs=[pl.BlockSpec((mb, D), lambda l, j: (0, 0)),
                  pl.BlockSpec((1, D, tn), lambda l, j: (l, 0, j)),
                  pl.BlockSpec((1, 1, tn), lambda l, j: (l, 0, j))],
        out_specs=pl.BlockSpec((1, mb, tn), lambda l, j: (l, 0, j)),
        out_shape=jax.ShapeDtypeStruct((depth, mb, n6), F32),
        compiler_params=_cparams(("arbitrary", "arbitrary")),
        name="mods",
    )(c_all, w_ada, b_ada.reshape(depth, 1, n6))


def _mod_specs(parts, tm, S, B, D):
    specs = []
    for p in parts:
        specs.append(pl.BlockSpec((1, 1, 1, D), lambda i, p=p: ((i * tm) // S, p, 0, 0)))
        specs.append(pl.BlockSpec((1, 1, 1, D), lambda i, p=p: (B, p, 0, 0)))
    return specs


def _modulate_kernel(x_ref, shl_ref, shc_ref, scl_ref, scc_ref, a_ref, *, S, L):
    tm = x_ref.shape[0]
    ctx = _is_ctx_rows(pl.program_id(0), tm, S, L)
    sh = jnp.where(ctx, shc_ref[0, 0], shl_ref[0, 0])
    sc = jnp.where(ctx, scc_ref[0, 0], scl_ref[0, 0])
    a_ref[...] = (x_ref[...] * (1.0 + sc) + sh).astype(a_ref.dtype)


def modulate(x, mod4, B, S, L, tm):
    T, D = x.shape
    return pl.pallas_call(
        functools.partial(_modulate_kernel, S=S, L=L),
        grid=(T // tm,),
        in_specs=[pl.BlockSpec((tm, D), lambda i: (i, 0))] + _mod_specs((0, 1), tm, S, B, D),
        out_specs=pl.BlockSpec((tm, D), lambda i: (i, 0)),
        out_shape=jax.ShapeDtypeStruct((T, D), BF16),
        compiler_params=_cparams(("parallel",)),
        name="modulate",
    )(x, mod4, mod4, mod4, mod4)


def _mm_kernel(a_ref, w_ref, *rest, quarter):
    o_ref = rest[-1]
    acc = _dot(a_ref[...], w_ref[...])
    if quarter:
        cos_ref, sin_ref = rest[0], rest[1]
        tn = acc.shape[1]
        lane = lax.broadcasted_iota(jnp.int32, acc.shape, 1)
        first = (lane % (2 * quarter)) < quarter
        rot = jnp.where(first, pltpu.roll(acc, tn - quarter, 1), pltpu.roll(acc, quarter, 1))
        acc = acc * cos_ref[...] + rot * sin_ref[...]
    o_ref[...] = acc.astype(o_ref.dtype)


def token_matmul(a, w, tm, tn, out_dtype, rope=None, S=None):
    T, K = a.shape
    N = w.shape[1]
    assert T % tm == 0 and N % tn == 0
    in_specs = [pl.BlockSpec((tm, K), lambda i, j: (i, 0)),
                pl.BlockSpec((K, tn), lambda i, j: (0, j))]
    args = [a, w]
    quarter = 0
    if rope is not None:
        cos, sin, quarter = rope
        nper = S // tm
        in_specs += [pl.BlockSpec((tm, tn), lambda i, j: (i % nper, 0)),
                     pl.BlockSpec((tm, tn), lambda i, j: (i % nper, 0))]
        args += [cos, sin]
    return pl.pallas_call(
        functools.partial(_mm_kernel, quarter=quarter),
        grid=(T // tm, N // tn),
        in_specs=in_specs,
        out_specs=pl.BlockSpec((tm, tn), lambda i, j: (i, j)),
        out_shape=jax.ShapeDtypeStruct((T, N), out_dtype),
        compiler_params=_cparams(("parallel", "arbitrary")),
        name="token_matmul",
    )(*args)


def rope_tables(L, C, d, tn):
    half, qt = d // 2, d // 4
    rows = L // GRID_W
    row = jnp.repeat(jnp.arange(rows), GRID_W)
    col = jnp.tile(jnp.arange(GRID_W), rows)
    inv = jnp.power(ROPE_BASE, -jnp.arange(0, half, 2, dtype=F32) / half)
    ang_r = row.astype(F32)[:, None] * inv[None, :]
    ang_c = col.astype(F32)[:, None] * inv[None, :]
    ang = jnp.concatenate([ang_r, ang_r, ang_c, ang_c], axis=-1)
    sign = jnp.where((jnp.arange(d) % half) < qt, -1.0, 1.0).astype(F32)
    cos = jnp.concatenate([jnp.cos(ang), jnp.ones((C, d), F32)], axis=0)
    sin = jnp.concatenate([jnp.sin(ang) * sign[None, :], jnp.zeros((C, d), F32)], axis=0)
    return jnp.tile(cos, (1, tn // d)), jnp.tile(sin, (1, tn // d)), qt


def _diff_attn_kernel(lam_ref, q_ref, k_ref, v_ref, nw_ref, o_ref, *, L, nlat):
    qi = pl.program_id(2)
    q = q_ref[0]
    k = k_ref[0]
    v = v_ref[0]
    tq, S = q.shape[0], k.shape[0]
    lane = lax.broadcasted_iota(jnp.int32, q.shape, 1)
    kidx = lax.broadcasted_iota(jnp.int32, (tq, S), 1)
    valid = kidx >= jnp.where(qi >= nlat, L, 0)
    scale = DA ** -0.5

    def probs(qm):
        s = jnp.where(valid, _dot_nt(qm, k) * scale, NEG)
        e = jnp.exp(s - jnp.max(s, axis=-1, keepdims=True))
        return e / jnp.sum(e, axis=-1, keepdims=True)

    p1 = probs(jnp.where(lane < DA, q, jnp.zeros_like(q)))
    p2 = probs(jnp.where(lane >= DA, q, jnp.zeros_like(q)))
    p = (p1 - lam_ref[0] * p2).astype(v.dtype)
    o = _dot(p, v)
    o = o * lax.rsqrt(jnp.mean(o * o, axis=-1, keepdims=True) + RMS_EPS) * nw_ref[...]
    o_ref[0] = o.astype(o_ref.dtype)


def diff_attention(qk_a, rest, lam, norm_w_scaled, B, S, L, HA, tq):
    nlat = L // tq
    return pl.pallas_call(
        functools.partial(_diff_attn_kernel, L=L, nlat=nlat),
        grid=(B, HA, S // tq),
        in_specs=[pl.BlockSpec(memory_space=pltpu.SMEM),
                  pl.BlockSpec((1, tq, LANE), lambda b, h, i: (b, i, h)),
                  pl.BlockSpec((1, S, LANE), lambda b, h, i: (b, 0, HA + h)),
                  pl.BlockSpec((1, S, LANE), lambda b, h, i: (b, 0, h)),
                  pl.BlockSpec((1, LANE), lambda b, h, i: (0, 0))],
        out_specs=pl.BlockSpec((1, tq, LANE), lambda b, h, i: (b, i, h)),
        out_shape=jax.ShapeDtypeStruct((B, S, HA * LANE), BF16),
        compiler_params=_cparams(("parallel", "parallel", "arbitrary")),
        name="diff_attention",
    )(lam, qk_a, qk_a, rest, norm_w_scaled)


def _swa_kernel(sink_ref, q_ref, kp_ref, kc_ref, kn_ref, kx_ref, vp_ref, vc_ref, vn_ref, vx_ref, o_ref,
                *, L, nb, G):
    h = pl.program_id(1)
    n = pl.program_id(2)
    q = jnp.concatenate([q_ref[0, :, g * DB:(g + 1) * DB] for g in range(G)], axis=0)
    k_loc = jnp.concatenate([kp_ref[0], kc_ref[0], kn_ref[0]], axis=0)
    v_loc = jnp.concatenate([vp_ref[0], vc_ref[0], vn_ref[0]], axis=0)
    rows = G * BLOCK
    scale = DB ** -0.5
    s_loc = _dot_nt(q, k_loc) * scale
    s_ctx = _dot_nt(q, kx_ref[0]) * scale
    r = lax.broadcasted_iota(jnp.int32, (rows, 3 * BLOCK), 0)
    j = lax.broadcasted_iota(jnp.int32, (rows, 3 * BLOCK), 1)
    qpos = n * BLOCK + r % BLOCK
    kpos = (n - 1) * BLOCK + j
    rel = kpos - qpos
    ok = (rel <= WINDOW) & (rel >= -WINDOW) & (kpos >= 0) & (kpos < L) & (n < nb)
    s_loc = jnp.where(ok, s_loc, NEG)
    g_of_row = lax.broadcasted_iota(jnp.int32, (rows, 1), 0) // BLOCK
    sink = jnp.zeros((rows, 1), F32)
    for g in range(G):
        sink = jnp.where(g_of_row == g, sink_ref[h * G + g], sink)
    m = jnp.maximum(jnp.maximum(jnp.max(s_loc, axis=-1, keepdims=True),
                                jnp.max(s_ctx, axis=-1, keepdims=True)), sink)
    e_loc = jnp.exp(s_loc - m)
    e_ctx = jnp.exp(s_ctx - m)
    den = (jnp.sum(e_loc, axis=-1, keepdims=True) + jnp.sum(e_ctx, axis=-1, keepdims=True)
           + jnp.exp(sink - m))
    inv = 1.0 / den
    o = _dot((e_loc * inv).astype(v_loc.dtype), v_loc) + _dot((e_ctx * inv).astype(v_loc.dtype), vx_ref[0])
    o_ref[0] = jnp.concatenate([o[g * BLOCK:(g + 1) * BLOCK] for g in range(G)], axis=1).astype(o_ref.dtype)


def swa_attention(qk_b, rest, sink, B, S, L, C, HB, v_blk0):
    G = HB // HB_KV
    nb = L // BLOCK
    xblk = L // C
    assert L % C == 0

    def loc(off, cblk):
        return lambda b, h, n: (b, jnp.clip(jnp.where(n < nb, n + off, 0), 0, nb - 1), cblk + h)

    kspec = lambda off: pl.BlockSpec((1, BLOCK, DB), loc(off, HB))
    vspec = lambda off: pl.BlockSpec((1, BLOCK, DB), loc(off, v_blk0))
    return pl.pallas_call(
        functools.partial(_swa_kernel, L=L, nb=nb, G=G),
        grid=(B, HB_KV, S // BLOCK),
        in_specs=[pl.BlockSpec(memory_space=pltpu.SMEM),
                  pl.BlockSpec((1, BLOCK, G * DB), lambda b, h, n: (b, n, h)),
                  kspec(-1), kspec(0), kspec(1),
                  pl.BlockSpec((1, C, DB), lambda b, h, n: (b, xblk, HB + h)),
                  vspec(-1), vspec(0), vspec(1),
                  pl.BlockSpec((1, C, DB), lambda b, h, n: (b, xblk, v_blk0 + h))],
        out_specs=pl.BlockSpec((1, BLOCK, G * DB), lambda b, h, n: (b, n, h)),
        out_shape=jax.ShapeDtypeStruct((B, S, HB * DB), BF16),
        compiler_params=_cparams(("parallel", "parallel", "arbitrary")),
        name="swa_attention",
    )(sink, qk_b, qk_b, qk_b, qk_b, qk_b, rest, rest, rest, rest)


def _delta_prep_kernel(x_ref, w_ref, o_ref, xs_ref, *, L, rc, mode):
    S, tc = x_ref.shape[1], x_ref.shape[2]
    pad = CONV_K // 2
    halo = 8
    xs_ref[0:halo, :] = jnp.zeros((halo, tc), F32)
    xs_ref[halo + S:halo + S + halo, :] = jnp.zeros((halo, tc), F32)
    xs_ref[halo:halo + S, :] = x_ref[0].astype(F32)
    w = w_ref[...]
    for r0 in range(0, S, rc):
        t = r0 + lax.broadcasted_iota(jnp.int32, (rc, 1), 0)
        acc = jnp.zeros((rc, tc), F32)
        for j in range(CONV_K):
            sl = xs_ref[halo + r0 + j - pad:halo + r0 + j - pad + rc, :]
            src = t + (j - pad)
            if r0 + rc == L and j > pad:
                sl = jnp.where(src < L, sl, 0.0)
            if r0 == L and j < pad:
                sl = jnp.where(src >= L, sl, 0.0)
            acc = acc + sl * w[j:j + 1, :]
        y = _silu(acc)
        if mode != "v":
            parts = []
            for h in range(tc // DKC):
                yh = y[:, h * DKC:(h + 1) * DKC]
                yh = yh * lax.rsqrt(jnp.sum(yh * yh, axis=-1, keepdims=True) + RMS_EPS)
                parts.append(yh * (DKC ** -0.5) if mode == "q" else yh)
            y = jnp.concatenate(parts, axis=1) if len(parts) > 1 else parts[0]
        o_ref[0, r0:r0 + rc, :] = y


def delta_prep(rest, conv_w_l, B, S, L, MIX_W, col_blk0, tc, rc):
    outs = []
    nt = MIX_W // tc
    for pi, mode in enumerate(("q", "k", "v")):
        outs.append(pl.pallas_call(
            functools.partial(_delta_prep_kernel, L=L, rc=rc, mode=mode),
            grid=(B, nt),
            in_specs=[pl.BlockSpec((1, S, tc), lambda b, j, pi=pi: (b, 0, col_blk0 + pi * nt + j)),
                      pl.BlockSpec((CONV_K, tc), lambda b, j, pi=pi: (0, pi * nt + j))],
            out_specs=pl.BlockSpec((1, S, tc), lambda b, j: (b, 0, j)),
            out_shape=jax.ShapeDtypeStruct((B, S, MIX_W), F32),
            scratch_shapes=[pltpu.VMEM((S + 16, tc), F32)],
            compiler_params=_cparams(("parallel", "parallel")),
            name="delta_prep_" + mode,
        )(rest, conv_w_l))
    return outs


def _unit_tri_inverse(a, ii, jj):
    same16 = (ii // 16) == (jj // 16)
    same32 = (ii // 32) == (jj // 32)
    eye = (ii == jj).astype(F32)
    d = jnp.where(same16, a, 0.0)
    l1 = jnp.where(same32 & jnp.logical_not(same16), a, 0.0)
    l2 = jnp.where(same32, 0.0, a)
    hp = functools.partial(_dot, precision=HIGHEST)
    d2 = hp(d, d)
    d4 = hp(d2, d2)
    d8 = hp(d4, d4)
    t = eye - d
    t = t + hp(t, d2)
    t = t + hp(t, d4)
    t = t + hp(t, d8)
    t = t - hp(hp(t, l1), t)
    t = t - hp(hp(t, l2), t)
    return t


def _delta_scan_kernel(q_ref, k_ref, v_ref, bac_ref, bar_ref, pc_ref, pr_ref, o_ref, s_ref, *, HC):
    d = pl.program_id(1)
    c = pl.program_id(2)

    @pl.when(c == 0)
    def _():
        s_ref[...] = jnp.zeros_like(s_ref)

    ii = lax.broadcasted_iota(jnp.int32, (CHUNK, CHUNK), 0)
    jj = lax.broadcasted_iota(jnp.int32, (CHUNK, CHUNK), 1)
    ahead = (ii - jj) * (1 - 2 * d)
    strict = ahead > 0
    incl = ahead >= 0
    tri = incl.astype(F32)

    bac = bac_ref[0, 0]
    pc = pc_ref[0]
    beta_c = jax.nn.sigmoid(bac)
    g_c = -jnp.exp(pc[0:1]) * _softplus(bac + pc[1:2])
    gc_c = _dot(tri, g_c, HIGHEST)
    gtot_c = jnp.sum(g_c, axis=0, keepdims=True)
    bar = bar_ref[0, 0, 0]
    pr = pr_ref[0]
    g_r = -jnp.exp(pr[:, 0:1]) * _softplus(bar + pr[:, 1:2])
    gc_r = _dot_nt(g_r, tri, HIGHEST)

    for h in range(HC):
        sl = slice(h * DKC, (h + 1) * DKC)
        q = q_ref[0, :, sl]
        k = k_ref[0, :, sl]
        v = v_ref[0, :, sl]
        beta = beta_c[:, h:h + 1]
        gcol = gc_c[:, HC + h:HC + h + 1]
        grow = gc_r[HC + h:HC + h + 1, :]
        gtot = gtot_c[:, HC + h:HC + h + 1]
        dec_incl = jnp.exp(jnp.where(incl, gcol - grow, NEG))
        dec_strict = jnp.where(strict, dec_incl, 0.0)
        egc = jnp.exp(gcol)
        kb = k * beta
        k16 = k.astype(BF16)
        a = _dot_nt(kb.astype(BF16), k16) * dec_strict
        t = _unit_tri_inverse(a, ii, jj)
        rhs = jnp.concatenate([v * beta, kb * egc], axis=1).astype(BF16)
        uw = _dot(t.astype(BF16), rhs)
        u0, w = uw[:, :DKC], uw[:, DKC:]
        qk = _dot_nt(q.astype(BF16), k16) * dec_incl
        state = s_ref[h]
        st16 = state.astype(BF16)
        ws_qs = _dot(jnp.concatenate([w, q * egc], axis=0).astype(BF16), st16)
        u = u0 - ws_qs[:CHUNK]
        u16 = u.astype(BF16)
        o = ws_qs[CHUNK:] + _dot(qk.astype(BF16), u16)
        k_tail = (k * jnp.exp(gtot - gcol)).astype(BF16)
        s_ref[h] = state * jnp.exp(gtot) + _dot_tn(k_tail, u16)
        o_ref[0, 0, :, sl] = o


def delta_scan(qc, kc, vc, bac, bar, pc, pr, B, S, L, HC):
    nch = S // CHUNK
    nl = L // CHUNK
    blk = lambda b, d, c: (b, jnp.where(d == 0, (c + nl) % nch, nch - 1 - c), 0)
    mw = HC * DKC
    return pl.pallas_call(
        functools.partial(_delta_scan_kernel, HC=HC),
        grid=(B, 2, nch),
        in_specs=[pl.BlockSpec((1, CHUNK, mw), blk),
                  pl.BlockSpec((1, CHUNK, mw), blk),
                  pl.BlockSpec((1, CHUNK, mw), blk),
                  pl.BlockSpec((1, 1, CHUNK, LANE), lambda b, d, c: (d,) + blk(b, d, c)),
                  pl.BlockSpec((1, 1, 1, 2 * HC, CHUNK), lambda b, d, c: (d,) + blk(b, d, c) + (0,)),
                  pl.BlockSpec((1, 8, LANE), lambda b, d, c: (d, 0, 0)),
                  pl.BlockSpec((1, 2 * HC, LANE), lambda b, d, c: (d, 0, 0))],
        out_specs=pl.BlockSpec((1, 1, CHUNK, mw), lambda b, d, c: (d,) + blk(b, d, c)),
        out_shape=jax.ShapeDtypeStruct((2, B, S, mw), F32),
        scratch_shapes=[pltpu.VMEM((HC, DKC, DKC), F32)],
        compiler_params=_cparams(("parallel", "arbitrary", "arbitrary")),
        name="delta_scan",
    )(qc, kc, vc, bac, bar, pc, pr)


def _delta_post_kernel(o_ref, z_ref, nw_ref, y_ref):
    o = o_ref[0] + o_ref[1]
    z = z_ref[...].astype(F32)
    nw = nw_ref[...]
    parts = []
    for h in range(o.shape[1] // DKC):
        oh = o[:, h * DKC:(h + 1) * DKC]
        oh = oh * lax.rsqrt(jnp.mean(oh * oh, axis=-1, keepdims=True) + RMS_EPS) * nw
        parts.append(oh * _silu(z[:, h * DKC:(h + 1) * DKC]))
    y_ref[...] = jnp.concatenate(parts, axis=1).astype(y_ref.dtype)


def delta_post(o2, rest, norm_w, T, MIX_W, z_blk, tm):
    return pl.pallas_call(
        _delta_post_kernel,
        grid=(T // tm,),
        in_specs=[pl.BlockSpec((2, tm, MIX_W), lambda i: (0, i, 0)),
                  pl.BlockSpec((tm, MIX_W), lambda i: (i, z_blk)),
                  pl.BlockSpec((1, DKC), lambda i: (0, 0))],
        out_specs=pl.BlockSpec((tm, MIX_W), lambda i: (i, 0)),
        out_shape=jax.ShapeDtypeStruct((T, MIX_W), BF16),
        compiler_params=_cparams(("parallel",)),
        name="delta_post",
    )(o2, rest, norm_w)


def _merge_kernel(oa_ref, ob_ref, oc_ref, w_ref, ga_ref, gb_ref, gc_ref, m_ref):
    acc = jax.nn.sigmoid(ga_ref[...].astype(F32)) * _dot(oa_ref[...], w_ref[0])
    acc = acc + jax.nn.sigmoid(gb_ref[...].astype(F32)) * _dot(ob_ref[...], w_ref[1])
    acc = acc + jax.nn.sigmoid(gc_ref[...].astype(F32)) * _dot(oc_ref[...], w_ref[2])
    m_ref[...] = acc.astype(m_ref.dtype)


def merge_branches(oa, ob, oc, w_branch, rest, T, D, MIX_W, gate_blk0, tm, tn):
    nd = D // tn
    o_spec = pl.BlockSpec((tm, MIX_W), lambda i, j: (i, 0))
    g_spec = lambda br: pl.BlockSpec((tm, tn), lambda i, j, br=br: (i, gate_blk0 + br * nd + j))
    return pl.pallas_call(
        _merge_kernel,
        grid=(T // tm, nd),
        in_specs=[o_spec, o_spec, o_spec,
                  pl.BlockSpec((3, MIX_W, tn), lambda i, j: (0, 0, j)),
                  g_spec(0), g_spec(1), g_spec(2)],
        out_specs=pl.BlockSpec((tm, tn), lambda i, j: (i, j)),
        out_shape=jax.ShapeDtypeStruct((T, D), BF16),
        compiler_params=_cparams(("parallel", "arbitrary")),
        name="merge_branches",
    )(oa, ob, oc, w_branch, rest, rest, rest)


def _outproj_ln_kernel(m_ref, w_ref, x_ref, g_ref, b_ref, gl_ref, gc_ref, shl_ref, shc_ref, scl_ref, scc_ref,
                       x1_ref, a2_ref, *, S, L, alpha):
    tm = x_ref.shape[0]
    ctx = _is_ctx_rows(pl.program_id(0), tm, S, L)
    gate = jnp.where(ctx, gc_ref[0, 0], gl_ref[0, 0])
    y = alpha * x_ref[...] + gate * _dot(m_ref[...], w_ref[...])
    x1 = _layer_norm(y, g_ref[...], b_ref[...])
    x1_ref[...] = x1
    sh = jnp.where(ctx, shc_ref[0, 0], shl_ref[0, 0])
    sc = jnp.where(ctx, scc_ref[0, 0], scl_ref[0, 0])
    a2_ref[...] = x1 * (1.0 + sc) + sh


def outproj_ln(merged, w_out, x, ln_g, ln_b, mod4, B, S, L, alpha, tm):
    T, D = x.shape
    row = pl.BlockSpec((tm, D), lambda i: (i, 0))
    vec = pl.BlockSpec((1, D), lambda i: (0, 0))
    return pl.pallas_call(
        functools.partial(_outproj_ln_kernel, S=S, L=L, alpha=alpha),
        grid=(T // tm,),
        in_specs=[row, pl.BlockSpec((D, D), lambda i: (0, 0)), row, vec, vec] + _mod_specs((2, 3, 4), tm, S, B, D),
        out_specs=[row, row],
        out_shape=[jax.ShapeDtypeStruct((T, D), F32), jax.ShapeDtypeStruct((T, D), F32)],
        compiler_params=_cparams(("parallel",)),
        name="outproj_ln",
    )(merged, w_out, x, ln_g, ln_b, *([mod4] * 6))


def _router_kernel(a_ref, w_ref, bias_ref, gate_ref, *, E):
    logits = _dot(a_ref[...], w_ref[...], HIGHEST)
    scores = jax.nn.sigmoid(logits)
    lane = lax.broadcasted_iota(jnp.int32, scores.shape, 1)
    work = jnp.where(lane < E, scores + bias_ref[...], NEG)
    chosen = jnp.zeros(scores.shape, jnp.bool_)
    for _ in range(TOP_K):
        m = jnp.max(work, axis=-1, keepdims=True)
        first = jnp.min(jnp.where(work == m, lane, LANE), axis=-1, keepdims=True)
        pick = lane == first
        chosen = jnp.logical_or(chosen, pick)
        work = jnp.where(pick, 2.0 * NEG, work)
    sel = jnp.where(chosen, scores, 0.0)
    gate_ref[...] = sel / jnp.sum(sel, axis=-1, keepdims=True) * ROUTED_SCALE


def router(a2, w_router_pad, bias_pad, E, tm):
    T, D = a2.shape
    return pl.pallas_call(
        functools.partial(_router_kernel, E=E),
        grid=(T // tm,),
        in_specs=[pl.BlockSpec((tm, D), lambda i: (i, 0)),
                  pl.BlockSpec((D, LANE), lambda i: (0, 0)),
                  pl.BlockSpec((1, LANE), lambda i: (0, 0))],
        out_specs=pl.BlockSpec((tm, LANE), lambda i: (i, 0)),
        out_shape=jax.ShapeDtypeStruct((T, LANE), F32),
        compiler_params=_cparams(("parallel",)),
        name="router",
    )(a2, w_router_pad, bias_pad)


def _experts_kernel(a_ref, gate_ref, wgu_ref, wd_ref, f_ref, acc_ref, *, DE):
    e = pl.program_id(1)

    @pl.when(e == 0)
    def _():
        acc_ref[...] = jnp.zeros_like(acc_ref)

    gu = _dot(a_ref[...], wgu_ref[0])
    lane = lax.broadcasted_iota(jnp.int32, gate_ref.shape, 1)
    gcol = jnp.sum(jnp.where(lane == e, gate_ref[...], 0.0), axis=-1, keepdims=True)
    hid = (_silu(gu[:, :DE]) * gu[:, DE:] * gcol).astype(BF16)
    acc_ref[...] += _dot(hid, wd_ref[0])

    @pl.when(e == pl.num_programs(1) - 1)
    def _():
        f_ref[...] = acc_ref[...]


def experts_dense(a2_16, gate, wgu, wd, tm):
    T, D = a2_16.shape
    E, _, de2 = wgu.shape
    return pl.pallas_call(
        functools.partial(_experts_kernel, DE=de2 // 2),
        grid=(T // tm, E),
        in_specs=[pl.BlockSpec((tm, D), lambda i, e: (i, 0)),
                  pl.BlockSpec((tm, LANE), lambda i, e: (i, 0)),
                  pl.BlockSpec((1, D, de2), lambda i, e: (e, 0, 0)),
                  pl.BlockSpec((1, de2 // 2, D), lambda i, e: (e, 0, 0))],
        out_specs=pl.BlockSpec((tm, D), lambda i, e: (i, 0)),
        out_shape=jax.ShapeDtypeStruct((T, D), F32),
        scratch_shapes=[pltpu.VMEM((tm, D), F32)],
        compiler_params=_cparams(("parallel", "arbitrary")),
        name="experts_dense",
    )(a2_16, gate, wgu, wd)


def _shared_ln_kernel(a_ref, f_ref, x_ref, wgu_ref, wd_ref, g_ref, b_ref, gl_ref, gc_ref, shl_ref, shc_ref,
                      scl_ref, scc_ref, x2_ref, a1_ref, *, S, L, alpha, DS):
    tm = x_ref.shape[0]
    ctx = _is_ctx_rows(pl.program_id(0), tm, S, L)
    gu = _dot(a_ref[...].astype(BF16), wgu_ref[...])
    f = f_ref[...] + _dot((_silu(gu[:, :DS]) * gu[:, DS:]).astype(BF16), wd_ref[...])
    gate = jnp.where(ctx, gc_ref[0, 0], gl_ref[0, 0])
    x2 = _layer_norm(alpha * x_ref[...] + gate * f, g_ref[...], b_ref[...])
    x2_ref[...] = x2
    sh = jnp.where(ctx, shc_ref[0, 0], shl_ref[0, 0])
    sc = jnp.where(ctx, scc_ref[0, 0], scl_ref[0, 0])
    a1_ref[...] = (x2 * (1.0 + sc) + sh).astype(a1_ref.dtype)


def shared_ln(a2, f_routed, x1, wgu_s, wd_s, ln_g, ln_b, mod4, mod4_next, B, S, L, alpha, tm):
    T, D = x1.shape
    ds2 = wgu_s.shape[1]
    row = pl.BlockSpec((tm, D), lambda i: (i, 0))
    vec = pl.BlockSpec((1, D), lambda i: (0, 0))
    return pl.pallas_call(
        functools.partial(_shared_ln_kernel, S=S, L=L, alpha=alpha, DS=ds2 // 2),
        grid=(T // tm,),
        in_specs=[row, row, row,
                  pl.BlockSpec((D, ds2), lambda i: (0, 0)),
                  pl.BlockSpec((ds2 // 2, D), lambda i: (0, 0)),
                  vec, vec] + _mod_specs((5,), tm, S, B, D) + _mod_specs((0, 1), tm, S, B, D),
        out_specs=[row, row],
        out_shape=[jax.ShapeDtypeStruct((T, D), F32), jax.ShapeDtypeStruct((T, D), BF16)],
        compiler_params=_cparams(("parallel",)),
        name="shared_ln",
    )(a2, f_routed, x1, wgu_s, wd_s, ln_g, ln_b, mod4, mod4, *([mod4_next] * 4))


def _col_layout(MIX_W, D, HC):
    kv = (MIX_W, MIX_W, HB_KV * DB, HB_KV * DB, MIX_W, MIX_W, MIX_W, 2 * HC, 2 * HC)
    qs = (MIX_W, MIX_W, MIX_W, 3 * D)
    names = ("ak", "av", "bk", "bv", "cq", "ck", "cv", "cb", "ca", "qa", "qb", "z", "gates")
    offs, o = {}, 0
    for n, s in zip(names, kv + qs):
        offs[n] = (o, o + s)
        o += s
    return offs


def kernel(x, c, ctx, c_ctx, w_ada, b_ada, w_in, lam_q1, lam_k1, lam_q2, lam_k2, diff_norm_w, sink, conv_w, a_log, dt_bias, delta_norm_w, w_branch, w_out, ln1_g, ln1_b, ln2_g, ln2_b, w_router, router_bias, w_gate_e, w_up_e, w_down_e, w_gate_s, w_up_s, w_down_s):
    B, L, D = x.shape
    C = ctx.shape[1]
    S = L + C
    T = B * S
    depth = w_ada.shape[0]
    MIX_W = D // 2
    HA = MIX_W // (2 * DA)
    HB = MIX_W // DB
    HC = MIX_W // DKC
    E = w_router.shape[-1]
    alpha = (2 * depth) ** 0.25
    assert 2 * HC <= LANE and E <= LANE

    tm_tok = math.gcd(256, math.gcd(L, C))
    tm_mm = S // 3 if S % 3 == 0 and (S // 3) % 16 == 0 else tm_tok
    tn_mm = min(512, MIX_W)
    tq = math.gcd(256, math.gcd(L, C))
    tc = min(512, MIX_W)
    rc = math.gcd(256, math.gcd(L, C))

    mb = -(-(B + 1) // 8) * 8
    c_all = jnp.zeros((mb, D), F32).at[:B].set(c).at[B].set(c_ctx)
    mods = compute_mods(c_all, w_ada, b_ada)
    mod4 = [mods[l].reshape(mb, 6, 1, D) for l in range(depth)]

    cos_a, sin_a, qt_a = rope_tables(L, C, DA, tn_mm)
    cos_b, sin_b, qt_b = rope_tables(L, C, DB, tn_mm if (HB + HB_KV) * DB % tn_mm == 0 else DB * HB_KV)
    tn_b = cos_b.shape[1]

    offs = _col_layout(MIX_W, D, HC)
    col = lambda w, n: w[:, offs[n][0]:offs[n][1]]
    n_rest = 5 * MIX_W + 3 * D + HB_KV * DB
    n_rest_pad = -(-n_rest // tn_mm) * tn_mm
    z_blk = 4
    gate_off = 5 * MIX_W
    bv_off = 5 * MIX_W + 3 * D

    h = jnp.concatenate([x, ctx], axis=1).reshape(T, D)
    a1 = modulate(h, mod4[0], B, S, L, tm_tok)

    for l in range(depth):
        wl = w_in[l]
        w_qk_a = jnp.concatenate([col(wl, "qa"), col(wl, "ak")], axis=1).astype(BF16)
        w_qk_b = jnp.concatenate([col(wl, "qb"), col(wl, "bk")], axis=1).astype(BF16)
        w_rest = jnp.concatenate([col(wl, n) for n in ("av", "cq", "ck", "cv", "z", "gates", "bv")]
                                 + [jnp.zeros((D, n_rest_pad - n_rest), F32)], axis=1).astype(BF16)
        w_ba = jnp.concatenate([col(wl, "cb"), col(wl, "ca"), jnp.zeros((D, LANE - 4 * HC), F32)],
                               axis=1).astype(BF16)

        qk_a = token_matmul(a1, w_qk_a, tm_mm, tn_mm, BF16, rope=(cos_a, sin_a, qt_a), S=S)
        qk_b = token_matmul(a1, w_qk_b, tm_mm, tn_b, BF16, rope=(cos_b, sin_b, qt_b), S=S)
        rest = token_matmul(a1, w_rest, tm_mm, tn_mm, BF16)
        ba = token_matmul(a1, w_ba, tm_mm, LANE, F32)

        lam_init = 0.8 - 0.6 * math.exp(-0.3 * l)
        lam = (jnp.exp(jnp.sum(lam_q1[l] * lam_k1[l])) - jnp.exp(jnp.sum(lam_q2[l] * lam_k2[l]))
               + lam_init).reshape(1).astype(F32)
        nw_a = (diff_norm_w[l] * (1.0 - lam_init)).reshape(1, 2 * DA)
        o_a = diff_attention(qk_a.reshape(B, S, -1), rest.reshape(B, S, -1), lam, nw_a, B, S, L, HA, tq)

        o_b = swa_attention(qk_b.reshape(B, S, -1), rest.reshape(B, S, -1), sink[l], B, S, L, C, HB,
                            bv_off // DB)

        qc, kc, vc = delta_prep(rest.reshape(B, S, -1), conv_w[l], B, S, L, MIX_W, MIX_W // tc, tc, rc)
        ba3 = ba.reshape(B, S, LANE)
        cb = ba3[..., :2 * HC].reshape(B, S, 2, HC)
        ca = ba3[..., 2 * HC:4 * HC].reshape(B, S, 2, HC)
        bdir = jnp.moveaxis(jnp.concatenate([cb, ca], axis=-1), 2, 0)
        bac = jnp.pad(bdir, ((0, 0), (0, 0), (0, 0), (0, LANE - 2 * HC)))
        bar = jnp.swapaxes(bdir.reshape(2, B, S // CHUNK, CHUNK, 2 * HC), -1, -2)
        pvec = jnp.stack([a_log[l], dt_bias[l]], axis=1).astype(F32)
        pc = jnp.zeros((2, 8, LANE), F32).at[:, :2, HC:2 * HC].set(pvec)
        pr = jnp.zeros((2, 2 * HC, LANE), F32).at[:, HC:, :2].set(jnp.swapaxes(pvec, 1, 2))
        o_c2 = delta_scan(qc, kc, vc, bac, bar, pc, pr, B, S, L, HC)
        o_c = delta_post(o_c2.reshape(2, T, MIX_W), rest, delta_norm_w[l].reshape(1, DKC), T, MIX_W, z_blk,
                         tm_tok)

        merged = merge_branches(o_a.reshape(T, MIX_W), o_b.reshape(T, MIX_W), o_c, w_branch[l].astype(BF16),
                                rest, T, D, MIX_W, gate_off // tn_mm, tm_mm, tn_mm)
        x1, a2 = outproj_ln(merged, w_out[l].astype(BF16), h, ln1_g[l].reshape(1, D), ln1_b[l].reshape(1, D),
                            mod4[l], B, S, L, alpha, tm_tok)

        w_r = jnp.pad(w_router[l], ((0, 0), (0, LANE - E)))
        b_r = jnp.pad(router_bias[l], (0, LANE - E)).reshape(1, LANE)
        gate = router(a2, w_r, b_r, E, tm_tok)
        wgu = jnp.concatenate([w_gate_e[l], w_up_e[l]], axis=-1).astype(BF16)
        a2_16 = a2.astype(BF16)
        f_routed = experts_dense(a2_16, gate, wgu, w_down_e[l].astype(BF16), tm_mm)
        wgu_s = jnp.concatenate([w_gate_s[l], w_up_s[l]], axis=-1).astype(BF16)
        h, a1 = shared_ln(a2, f_routed, x1, wgu_s, w_down_s[l].astype(BF16), ln2_g[l].reshape(1, D),
                          ln2_b[l].reshape(1, D), mod4[l], mod4[min(l + 1, depth - 1)], B, S, L, alpha, tm_tok)

    return h.reshape(B, S, D)[:, :L]
```

```python
import functools
import math

import jax
import jax.numpy as jnp
from jax import lax
from jax.experimental import pallas as pl
from jax.experimental.pallas import tpu as pltpu

F32 = jnp.float32
BF16 = jnp.bfloat16
HIGHEST = lax.Precision.HIGHEST

LANE = 128
DA = 64
DB = 128
HB_KV = 2
DKC = 128
CONV_K = 5
CHUNK = 64
BLOCK = 128
WINDOW = 128
GRID_W = 64
ROPE_BASE = 10000.0
TOP_K = 8
ROUTED_SCALE = 2.5
LN_EPS = 1e-5
RMS_EPS = 1e-6
NEG = -1e30
VMEM_LIMIT = 56 * 1024 * 1024


def _cparams(sem):
    return pltpu.CompilerParams(dimension_semantics=sem, vmem_limit_bytes=VMEM_LIMIT)


def _dot(a, b, precision=None):
    return jnp.dot(a, b, preferred_element_type=F32, precision=precision)


def _dot_nt(a, b, precision=None):
    return lax.dot_general(a, b, (((1,), (1,)), ((), ())), preferred_element_type=F32, precision=precision)


def _dot_tn(a, b, precision=None):
    return lax.dot_general(a, b, (((0,), (0,)), ((), ())), preferred_element_type=F32, precision=precision)


def _silu(x):
    return x * jax.nn.sigmoid(x)


def _softplus(x):
    return jnp.maximum(x, 0.0) + jnp.log1p(jnp.exp(-jnp.abs(x)))


def _layer_norm(y, g, b):
    mu = jnp.mean(y, axis=-1, keepdims=True)
    yc = y - mu
    var = jnp.mean(yc * yc, axis=-1, keepdims=True)
    return yc * lax.rsqrt(var + LN_EPS) * g + b


def _pick(tile, S, L):
    assert S % tile == 0, (S, tile)
    return tile


def _is_ctx_rows(i, tm, S, L):
    row = (i * tm) % S + lax.broadcasted_iota(jnp.int32, (tm, 1), 0)
    return row >= L


def _mods_kernel(c_ref, w_ref, b_ref, o_ref):
    o_ref[0] = _dot(_silu(c_ref[...]), w_ref[0], HIGHEST) + b_ref[0]


def compute_mods(c_all, w_ada, b_ada):
    depth, D, n6 = w_ada.shape
    mb = c_all.shape[0]
    tn = 1024
    return pl.pallas_call(
        _mods_kernel,
        grid=(depth, n6 // tn),
        in_specs=[pl.BlockSpec((mb, D), lambda l, j: (0, 0)),
                  pl.BlockSpec((1, D, tn), lambda l, j: (l, 0, j)),
                  pl.BlockSpec((1, 1, tn), lambda l, j: (l, 0, j))],
        out_specs=pl.BlockSpec((1, mb, tn), lambda l, j: (l, 0, j)),
        out_shape=jax.ShapeDtypeStruct((depth, mb, n6), F32),
        compiler_params=_cparams(("arbitrary", "arbitrary")),
        name="mods",
    )(c_all, w_ada, b_ada.reshape(depth, 1, n6))


def _mod_specs(parts, tm, S, B, D):
    specs = []
    for p in parts:
        specs.append(pl.BlockSpec((1, 1, 1, D), lambda i, p=p: ((i * tm) // S, p, 0, 0)))
        specs.append(pl.BlockSpec((1, 1, 1, D), lambda i, p=p: (B, p, 0, 0)))
    return specs


def _modulate_kernel(x_ref, shl_ref, shc_ref, scl_ref, scc_ref, a_ref, *, S, L):
    tm = x_ref.shape[0]
    ctx = _is_ctx_rows(pl.program_id(0), tm, S, L)
    sh = jnp.where(ctx, shc_ref[0, 0], shl_ref[0, 0])
    sc = jnp.where(ctx, scc_ref[0, 0], scl_ref[0, 0])
    a_ref[...] = (x_ref[...] * (1.0 + sc) + sh).astype(a_ref.dtype)


def modulate(x, mod4, B, S, L, tm):
    T, D = x.shape
    return pl.pallas_call(
        functools.partial(_modulate_kernel, S=S, L=L),
        grid=(T // tm,),
        in_specs=[pl.BlockSpec((tm, D), lambda i: (i, 0))] + _mod_specs((0, 1), tm, S, B, D),
        out_specs=pl.BlockSpec((tm, D), lambda i: (i, 0)),
        out_shape=jax.ShapeDtypeStruct((T, D), BF16),
        compiler_params=_cparams(("parallel",)),
        name="modulate",
    )(x, mod4, mod4, mod4, mod4)


def _mm_kernel(a_ref, w_ref, *rest, quarter):
    o_ref = rest[-1]
    acc = _dot(a_ref[...], w_ref[...])
    if quarter:
        cos_ref, sin_ref = rest[0], rest[1]
        tn = acc.shape[1]
        lane = lax.broadcasted_iota(jnp.int32, acc.shape, 1)
        first = (lane % (2 * quarter)) < quarter
        rot = jnp.where(first, pltpu.roll(acc, tn - quarter, 1), pltpu.roll(acc, quarter, 1))
        acc = acc * cos_ref[...] + rot * sin_ref[...]
    o_ref[...] = acc.astype(o_ref.dtype)


def token_matmul(a, w, tm, tn, out_dtype, rope=None, S=None):
    T, K = a.shape
    N = w.shape[1]
    assert T % tm == 0 and N % tn == 0
    in_specs = [pl.BlockSpec((tm, K), lambda i, j: (i, 0)),
                pl.BlockSpec((K, tn), lambda i, j: (0, j))]
    args = [a, w]
    quarter = 0
    if rope is not None:
        cos, sin, quarter = rope
        nper = S // tm
        in_specs += [pl.BlockSpec((tm, tn), lambda i, j: (i % nper, 0)),
                     pl.BlockSpec((tm, tn), lambda i, j: (i % nper, 0))]
        args += [cos, sin]
    return pl.pallas_call(
        functools.partial(_mm_kernel, quarter=quarter),
        grid=(T // tm, N // tn),
        in_specs=in_specs,
        out_specs=pl.BlockSpec((tm, tn), lambda i, j: (i, j)),
        out_shape=jax.ShapeDtypeStruct((T, N), out_dtype),
        compiler_params=_cparams(("parallel", "arbitrary")),
        name="token_matmul",
    )(*args)


def rope_tables(L, C, d, tn):
    half, qt = d // 2, d // 4
    rows = L // GRID_W
    row = jnp.repeat(jnp.arange(rows), GRID_W)
    col = jnp.tile(jnp.arange(GRID_W), rows)
    inv = jnp.power(ROPE_BASE, -jnp.arange(0, half, 2, dtype=F32) / half)
    ang_r = row.astype(F32)[:, None] * inv[None, :]
    ang_c = col.astype(F32)[:, None] * inv[None, :]
    ang = jnp.concatenate([ang_r, ang_r, ang_c, ang_c], axis=-1)
    sign = jnp.where((jnp.arange(d) % half) < qt, -1.0, 1.0).astype(F32)
    cos = jnp.concatenate([jnp.cos(ang), jnp.ones((C, d), F32)], axis=0)
    sin = jnp.concatenate([jnp.sin(ang) * sign[None, :], jnp.zeros((C, d), F32)], axis=0)
    return jnp.tile(cos, (1, tn // d)), jnp.tile(sin, (1, tn // d)), qt


def _diff_attn_kernel(lam_ref, q_ref, k_ref, v_ref, nw_ref, o_ref, *, L, nlat):
    qi = pl.program_id(2)
    q = q_ref[0] * (DA ** -0.5)
    lane = lax.broadcasted_iota(jnp.int32, q.shape, 1)
    q1 = jnp.where(lane < DA, q, jnp.zeros_like(q))
    q2 = jnp.where(lane >= DA, q, jnp.zeros_like(q))

    def attend(k, v):
        def one_map(qm):
            s = _dot_nt(qm, k)
            e = jnp.exp(s - jnp.max(s, axis=-1, keepdims=True))
            return _dot(e.astype(v.dtype), v) / jnp.sum(e, axis=-1, keepdims=True)

        o = one_map(q1) - lam_ref[0] * one_map(q2)
        o = o * lax.rsqrt(jnp.mean(o * o, axis=-1, keepdims=True) + RMS_EPS) * nw_ref[...]
        o_ref[0] = o.astype(o_ref.dtype)

    @pl.when(qi < nlat)
    def _():
        attend(k_ref[0], v_ref[0])

    @pl.when(qi >= nlat)
    def _():
        attend(k_ref[0, L:, :], v_ref[0, L:, :])


def diff_attention(qk_a, rest, lam, norm_w_scaled, B, S, L, HA, tq):
    nlat = L // tq
    return pl.pallas_call(
        functools.partial(_diff_attn_kernel, L=L, nlat=nlat),
        grid=(B, HA, S // tq),
        in_specs=[pl.BlockSpec(memory_space=pltpu.SMEM),
                  pl.BlockSpec((1, tq, LANE), lambda b, h, i: (b, i, h)),
                  pl.BlockSpec((1, S, LANE), lambda b, h, i: (b, 0, HA + h)),
                  pl.BlockSpec((1, S, LANE), lambda b, h, i: (b, 0, h)),
                  pl.BlockSpec((1, LANE), lambda b, h, i: (0, 0))],
        out_specs=pl.BlockSpec((1, tq, LANE), lambda b, h, i: (b, i, h)),
        out_shape=jax.ShapeDtypeStruct((B, S, HA * LANE), BF16),
        compiler_params=_cparams(("parallel", "parallel", "arbitrary")),
        name="diff_attention",
    )(lam, qk_a, qk_a, rest, norm_w_scaled)


def _swa_kernel(sink_ref, q_ref, kp_ref, kc_ref, kn_ref, kx_ref, vp_ref, vc_ref, vn_ref, vx_ref, o_ref,
                *, L, nb, G):
    h = pl.program_id(1)
    n = pl.program_id(2)
    q = jnp.concatenate([q_ref[0, :, g * DB:(g + 1) * DB] for g in range(G)], axis=0)
    k_loc = jnp.concatenate([kp_ref[0], kc_ref[0], kn_ref[0]], axis=0)
    v_loc = jnp.concatenate([vp_ref[0], vc_ref[0], vn_ref[0]], axis=0)
    rows = G * BLOCK
    scale = DB ** -0.5
    s_loc = _dot_nt(q, k_loc) * scale
    s_ctx = _dot_nt(q, kx_ref[0]) * scale
    r = lax.broadcasted_iota(jnp.int32, (rows, 3 * BLOCK), 0)
    j = lax.broadcasted_iota(jnp.int32, (rows, 3 * BLOCK), 1)
    qpos = n * BLOCK + r % BLOCK
    kpos = (n - 1) * BLOCK + j
    rel = kpos - qpos
    ok = (rel <= WINDOW) & (rel >= -WINDOW) & (kpos >= 0) & (kpos < L) & (n < nb)
    s_loc = jnp.where(ok, s_loc, NEG)
    g_of_row = lax.broadcasted_iota(jnp.int32, (rows, 1), 0) // BLOCK
    sink = jnp.zeros((rows, 1), F32)
    for g in range(G):
        sink = jnp.where(g_of_row == g, sink_ref[h * G + g], sink)
    m = jnp.maximum(jnp.maximum(jnp.max(s_loc, axis=-1, keepdims=True),
                                jnp.max(s_ctx, axis=-1, keepdims=True)), sink)
    e_loc = jnp.exp(s_loc - m)
    e_ctx = jnp.exp(s_ctx - m)
    den = (jnp.sum(e_loc, axis=-1, keepdims=True) + jnp.sum(e_ctx, axis=-1, keepdims=True)
           + jnp.exp(sink - m))
    inv = 1.0 / den
    o = _dot((e_loc * inv).astype(v_loc.dtype), v_loc) + _dot((e_ctx * inv).astype(v_loc.dtype), vx_ref[0])
    o_ref[0] = jnp.concatenate([o[g * BLOCK:(g + 1) * BLOCK] for g in range(G)], axis=1).astype(o_ref.dtype)


def swa_attention(qk_b, rest, sink, B, S, L, C, HB, v_blk0):
    G = HB // HB_KV
    nb = L // BLOCK
    xblk = L // C
    assert L % C == 0

    def loc(off, cblk):
        return lambda b, h, n: (b, jnp.clip(jnp.where(n < nb, n + off, 0), 0, nb - 1), cblk + h)

    kspec = lambda off: pl.BlockSpec((1, BLOCK, DB), loc(off, HB))
    vspec = lambda off: pl.BlockSpec((1, BLOCK, DB), loc(off, v_blk0))
    return pl.pallas_call(
        functools.partial(_swa_kernel, L=L, nb=nb, G=G),
        grid=(B, HB_KV, S // BLOCK),
        in_specs=[pl.BlockSpec(memory_space=pltpu.SMEM),
                  pl.BlockSpec((1, BLOCK, G * DB), lambda b, h, n: (b, n, h)),
                  kspec(-1), kspec(0), kspec(1),
                  pl.BlockSpec((1, C, DB), lambda b, h, n: (b, xblk, HB + h)),
                  vspec(-1), vspec(0), vspec(1),
                  pl.BlockSpec((1, C, DB), lambda b, h, n: (b, xblk, v_blk0 + h))],
        out_specs=pl.BlockSpec((1, BLOCK, G * DB), lambda b, h, n: (b, n, h)),
        out_shape=jax.ShapeDtypeStruct((B, S, HB * DB), BF16),
        compiler_params=_cparams(("parallel", "parallel", "arbitrary")),
        name="swa_attention",
    )(sink, qk_b, qk_b, qk_b, qk_b, qk_b, rest, rest, rest, rest)


def _delta_prep_kernel(x_ref, w_ref, o_ref, xs_ref, *, L, rc, mode):
    S, tc = x_ref.shape[1], x_ref.shape[2]
    pad = CONV_K // 2
    halo = 8
    xs_ref[0:halo, :] = jnp.zeros((halo, tc), F32)
    xs_ref[halo + S:halo + S + halo, :] = jnp.zeros((halo, tc), F32)
    xs_ref[halo:halo + S, :] = x_ref[0].astype(F32)
    w = w_ref[...]
    for r0 in range(0, S, rc):
        t = r0 + lax.broadcasted_iota(jnp.int32, (rc, 1), 0)
        acc = jnp.zeros((rc, tc), F32)
        for j in range(CONV_K):
            sl = xs_ref[halo + r0 + j - pad:halo + r0 + j - pad + rc, :]
            src = t + (j - pad)
            if r0 + rc == L and j > pad:
                sl = jnp.where(src < L, sl, 0.0)
            if r0 == L and j < pad:
                sl = jnp.where(src >= L, sl, 0.0)
            acc = acc + sl * w[j:j + 1, :]
        y = _silu(acc)
        if mode != "v":
            parts = []
            for h in range(tc // DKC):
                yh = y[:, h * DKC:(h + 1) * DKC]
                yh = yh * lax.rsqrt(jnp.sum(yh * yh, axis=-1, keepdims=True) + RMS_EPS)
                parts.append(yh * (DKC ** -0.5) if mode == "q" else yh)
            y = jnp.concatenate(parts, axis=1) if len(parts) > 1 else parts[0]
        o_ref[0, r0:r0 + rc, :] = y


def delta_prep(rest, conv_w_l, B, S, L, MIX_W, col_blk0, tc, rc):
    outs = []
    nt = MIX_W // tc
    for pi, mode in enumerate(("q", "k", "v")):
        outs.append(pl.pallas_call(
            functools.partial(_delta_prep_kernel, L=L, rc=rc, mode=mode),
            grid=(B, nt),
            in_specs=[pl.BlockSpec((1, S, tc), lambda b, j, pi=pi: (b, 0, col_blk0 + pi * nt + j)),
                      pl.BlockSpec((CONV_K, tc), lambda b, j, pi=pi: (0, pi * nt + j))],
            out_specs=pl.BlockSpec((1, S, tc), lambda b, j: (b, 0, j)),
            out_shape=jax.ShapeDtypeStruct((B, S, MIX_W), F32),
            scratch_shapes=[pltpu.VMEM((S + 16, tc), F32)],
            compiler_params=_cparams(("parallel", "parallel")),
            name="delta_prep_" + mode,
        )(rest, conv_w_l))
    return outs


def _split16(x):
    hi = x.astype(BF16)
    return hi, (x - hi.astype(F32)).astype(BF16)


def _prod2(x, y):
    return _dot(jnp.concatenate(x, axis=1), jnp.concatenate([y[0], y[1], y[0], y[1]], axis=0))


def _unit_tri_inverse(a_list, ii, jj):
    same16 = (ii // 16) == (jj // 16)
    same32 = (ii // 32) == (jj // 32)
    eye = (ii == jj).astype(F32)
    each = lambda f, *ls: [f(*xs) for xs in zip(*ls)]
    d = each(lambda a: _split16(jnp.where(same16, a, 0.0)), a_list)
    l1 = each(lambda a: _split16(jnp.where(same32 & jnp.logical_not(same16), a, 0.0)), a_list)
    l2 = each(lambda a: _split16(jnp.where(same32, 0.0, a)), a_list)
    d2 = each(lambda x: _split16(_prod2(x, x)), d)
    d4 = each(lambda x: _split16(_prod2(x, x)), d2)
    d8 = each(lambda x: _split16(_prod2(x, x)), d4)
    t = each(lambda a: eye - jnp.where(same16, a, 0.0), a_list)
    for p in (d2, d4, d8):
        t = each(lambda tt, pp: tt + _prod2(_split16(tt), pp), t, p)
    for l in (l1, l2):
        ts = each(_split16, t)
        m = each(lambda tt, ll: _split16(_prod2(tt, ll)), ts, l)
        t = each(lambda tf, mm, tt: tf - _prod2(mm, tt), t, m, ts)
    return t


def _delta_scan_kernel(qf_ref, kf_ref, vf_ref, qb_ref, kb_ref, vb_ref, bacf_ref, bacb_ref, barf_ref, barb_ref,
                       pc_ref, pr_ref, of_ref, ob_ref, s_ref, *, HC):
    @pl.when(pl.program_id(1) == 0)
    def _():
        s_ref[...] = jnp.zeros_like(s_ref)

    ii = lax.broadcasted_iota(jnp.int32, (CHUNK, 2 * CHUNK), 0)
    jj = lax.broadcasted_iota(jnp.int32, (CHUNK, 2 * CHUNK), 1) % CHUNK
    dirs = ((qf_ref, kf_ref, vf_ref, bacf_ref, barf_ref, of_ref), (qb_ref, kb_ref, vb_ref, bacb_ref, barb_ref, ob_ref))
    chains = []
    for dn, (q_ref, k_ref, v_ref, bac_ref, bar_ref, o_ref) in enumerate(dirs):
        ahead = (ii - jj) if dn == 0 else (jj - ii)
        strict = ahead > 0
        incl = ahead >= 0
        tri2 = incl.astype(F32)
        tri = tri2[:, :CHUNK]
        bac = bac_ref[0, 0]
        pc = pc_ref[dn]
        beta_c = jax.nn.sigmoid(bac)
        g_c = -jnp.exp(pc[0:1]) * _softplus(bac + pc[1:2])
        gc_c = _dot(tri, g_c, HIGHEST)
        gtot_c = jnp.sum(g_c, axis=0, keepdims=True)
        bar = bar_ref[0, 0, 0]
        pr = pr_ref[dn]
        g_r = -jnp.exp(pr[:, 0:1]) * _softplus(bar + pr[:, 1:2])
        gc_r = _dot_nt(g_r, jnp.concatenate([tri, tri], axis=0), HIGHEST)
        for h in range(HC):
            chains.append(dict(dn=dn, h=h, q_ref=q_ref, k_ref=k_ref, v_ref=v_ref, o_ref=o_ref, strict=strict,
                               incl=incl, beta=beta_c[:, h:h + 1], gcol=gc_c[:, HC + h:HC + h + 1],
                               grow=gc_r[HC + h:HC + h + 1, :], gtot=gtot_c[:, HC + h:HC + h + 1]))

    sl = lambda ch: slice(ch["h"] * DKC, (ch["h"] + 1) * DKC)
    for ch in chains:
        ch["k"] = ch["k_ref"][0, :, sl(ch)]
        ch["k16"] = ch["k"].astype(BF16)
        ch["kb"] = ch["k"] * ch["beta"]
        ch["dec"] = jnp.exp(jnp.where(ch["incl"], ch["gcol"] - ch["grow"], NEG))
        ch["egc"] = jnp.exp(ch["gcol"])
    for ch in chains:
        k2 = jnp.concatenate([ch["k16"], ch["k16"]], axis=0)
        ch["a"] = _dot_nt(ch["kb"].astype(BF16), k2) * jnp.where(ch["strict"], ch["dec"], 0.0)
    t_list = _unit_tri_inverse([ch["a"] for ch in chains], ii, jj)
    for ch, t in zip(chains, t_list):
        v = ch["v_ref"][0, :, sl(ch)]
        rhs = jnp.concatenate([v * ch["beta"], ch["kb"] * ch["egc"]], axis=1).astype(BF16)
        ch["uw"] = _dot(t[:, :CHUNK].astype(BF16), rhs)
    for ch in chains:
        q = ch["q_ref"][0, :, sl(ch)]
        ch["qk"] = (_dot_nt(q.astype(BF16), ch["k16"]) * ch["dec"][:, :CHUNK]).astype(BF16)
        ch["state"] = s_ref[ch["dn"], ch["h"]]
        lhs = jnp.concatenate([ch["uw"][:, DKC:], q * ch["egc"]], axis=0).astype(BF16)
        ch["ws_qs"] = _dot(lhs, ch["state"].astype(BF16))
    for ch in chains:
        ch["u16"] = (ch["uw"][:, :DKC] - ch["ws_qs"][:CHUNK]).astype(BF16)
    for ch in chains:
        ch["o_ref"][0, :, sl(ch)] = ch["ws_qs"][CHUNK:] + _dot(ch["qk"], ch["u16"])
        k_tail = (ch["k"] * jnp.exp(ch["gtot"] - ch["gcol"])).astype(BF16)
        s_ref[ch["dn"], ch["h"]] = ch["state"] * jnp.exp(ch["gtot"]) + _dot_tn(k_tail, ch["u16"])


def delta_scan(qc, kc, vc, bac, bar, pc, pr, B, S, L, HC):
    nch = S // CHUNK
    nl = L // CHUNK
    fwd = lambda b, c: (b, (c + nl) % nch, 0)
    bwd = lambda b, c: (b, nch - 1 - c, 0)
    mw = HC * DKC
    row = lambda m: pl.BlockSpec((1, CHUNK, mw), m)
    bac_spec = lambda dn, m: pl.BlockSpec((1, 1, CHUNK, LANE), lambda b, c: (dn,) + m(b, c))
    bar_spec = lambda dn, m: pl.BlockSpec((1, 1, 1, 2 * HC, CHUNK), lambda b, c: (dn,) + m(b, c) + (0,))
    return pl.pallas_call(
        functools.partial(_delta_scan_kernel, HC=HC),
        grid=(B, nch),
        in_specs=[row(fwd), row(fwd), row(fwd), row(bwd), row(bwd), row(bwd),
                  bac_spec(0, fwd), bac_spec(1, bwd), bar_spec(0, fwd), bar_spec(1, bwd),
                  pl.BlockSpec((2, 8, LANE), lambda b, c: (0, 0, 0)),
                  pl.BlockSpec((2, 2 * HC, LANE), lambda b, c: (0, 0, 0))],
        out_specs=[row(fwd), row(bwd)],
        out_shape=[jax.ShapeDtypeStruct((B, S, mw), F32), jax.ShapeDtypeStruct((B, S, mw), F32)],
        scratch_shapes=[pltpu.VMEM((2, HC, DKC, DKC), F32)],
        compiler_params=_cparams(("parallel", "arbitrary")),
        name="delta_scan",
    )(qc, kc, vc, qc, kc, vc, bac, bac, bar, bar, pc, pr)


def _delta_post_kernel(of_ref, ob_ref, z_ref, nw_ref, y_ref):
    o = of_ref[...] + ob_ref[...]
    z = z_ref[...].astype(F32)
    nw = nw_ref[...]
    parts = []
    for h in range(o.shape[1] // DKC):
        oh = o[:, h * DKC:(h + 1) * DKC]
        oh = oh * lax.rsqrt(jnp.mean(oh * oh, axis=-1, keepdims=True) + RMS_EPS) * nw
        parts.append(oh * _silu(z[:, h * DKC:(h + 1) * DKC]))
    y_ref[...] = jnp.concatenate(parts, axis=1).astype(y_ref.dtype)


def delta_post(o_f, o_b, rest, norm_w, T, MIX_W, z_blk, tm):
    return pl.pallas_call(
        _delta_post_kernel,
        grid=(T // tm,),
        in_specs=[pl.BlockSpec((tm, MIX_W), lambda i: (i, 0)),
                  pl.BlockSpec((tm, MIX_W), lambda i: (i, 0)),
                  pl.BlockSpec((tm, MIX_W), lambda i: (i, z_blk)),
                  pl.BlockSpec((1, DKC), lambda i: (0, 0))],
        out_specs=pl.BlockSpec((tm, MIX_W), lambda i: (i, 0)),
        out_shape=jax.ShapeDtypeStruct((T, MIX_W), BF16),
        compiler_params=_cparams(("parallel",)),
        name="delta_post",
    )(o_f, o_b, rest, norm_w)


def _merge_kernel(oa_ref, ob_ref, oc_ref, w_ref, ga_ref, gb_ref, gc_ref, m_ref):
    acc = jax.nn.sigmoid(ga_ref[...].astype(F32)) * _dot(oa_ref[...], w_ref[0])
    acc = acc + jax.nn.sigmoid(gb_ref[...].astype(F32)) * _dot(ob_ref[...], w_ref[1])
    acc = acc + jax.nn.sigmoid(gc_ref[...].astype(F32)) * _dot(oc_ref[...], w_ref[2])
    m_ref[...] = acc.astype(m_ref.dtype)


def merge_branches(oa, ob, oc, w_branch, rest, T, D, MIX_W, gate_blk0, tm, tn):
    nd = D // tn
    o_spec = pl.BlockSpec((tm, MIX_W), lambda i, j: (i, 0))
    g_spec = lambda br: pl.BlockSpec((tm, tn), lambda i, j, br=br: (i, gate_blk0 + br * nd + j))
    return pl.pallas_call(
        _merge_kernel,
        grid=(T // tm, nd),
        in_specs=[o_spec, o_spec, o_spec,
                  pl.BlockSpec((3, MIX_W, tn), lambda i, j: (0, 0, j)),
                  g_spec(0), g_spec(1), g_spec(2)],
        out_specs=pl.BlockSpec((tm, tn), lambda i, j: (i, j)),
        out_shape=jax.ShapeDtypeStruct((T, D), BF16),
        compiler_params=_cparams(("parallel", "arbitrary")),
        name="merge_branches",
    )(oa, ob, oc, w_branch, rest, rest, rest)


def _pack_rows(y):
    half = y.shape[1] // 2

    def rne_bits(v):
        b = lax.bitcast_convert_type(v, jnp.uint32)
        return b + jnp.uint32(0x7FFF) + ((b >> 16) & jnp.uint32(1))

    return (rne_bits(y[:, half:]) & jnp.uint32(0xFFFF0000)) | (rne_bits(y[:, :half]) >> 16)


def _unpack_rows(w):
    lo = lax.bitcast_convert_type(w << 16, F32)
    hi = lax.bitcast_convert_type(w & jnp.uint32(0xFFFF0000), F32)
    return lo, hi


def _outproj_ln_kernel(m_ref, w_ref, x_ref, g_ref, b_ref, gl_ref, gc_ref, shl_ref, shc_ref, scl_ref, scc_ref,
                       x1_ref, a2_ref, a2p_ref, *, S, L, alpha):
    tm = x_ref.shape[0]
    ctx = _is_ctx_rows(pl.program_id(0), tm, S, L)
    gate = jnp.where(ctx, gc_ref[0, 0], gl_ref[0, 0])
    y = alpha * x_ref[...] + gate * _dot(m_ref[...], w_ref[...])
    x1 = _layer_norm(y, g_ref[...], b_ref[...])
    x1_ref[...] = x1
    sh = jnp.where(ctx, shc_ref[0, 0], shl_ref[0, 0])
    sc = jnp.where(ctx, scc_ref[0, 0], scl_ref[0, 0])
    a2 = x1 * (1.0 + sc) + sh
    a2_ref[...] = a2
    a2p_ref[...] = _pack_rows(a2)


def outproj_ln(merged, w_out, x, ln_g, ln_b, mod4, B, S, L, alpha, tm):
    T, D = x.shape
    row = pl.BlockSpec((tm, D), lambda i: (i, 0))
    vec = pl.BlockSpec((1, D), lambda i: (0, 0))
    return pl.pallas_call(
        functools.partial(_outproj_ln_kernel, S=S, L=L, alpha=alpha),
        grid=(T // tm,),
        in_specs=[row, pl.BlockSpec((D, D), lambda i: (0, 0)), row, vec, vec] + _mod_specs((2, 3, 4), tm, S, B, D),
        out_specs=[row, row, pl.BlockSpec((tm, D // 2), lambda i: (i, 0))],
        out_shape=[jax.ShapeDtypeStruct((T, D), F32), jax.ShapeDtypeStruct((T, D), F32),
                   jax.ShapeDtypeStruct((T, D // 2), jnp.uint32)],
        compiler_params=_cparams(("parallel",)),
        name="outproj_ln",
    )(merged, w_out, x, ln_g, ln_b, *([mod4] * 6))


def _router_kernel(a_ref, w_ref, bias_ref, gate_ref, idx_ref, *, E):
    logits = _dot(a_ref[...], w_ref[...], HIGHEST)
    scores = jax.nn.sigmoid(logits)
    lane = lax.broadcasted_iota(jnp.int32, scores.shape, 1)
    work = jnp.where(lane < E, scores + bias_ref[...], NEG)
    chosen = jnp.zeros(scores.shape, jnp.bool_)
    idx = jnp.zeros(scores.shape, jnp.int32)
    for kk in range(TOP_K):
        m = jnp.max(work, axis=-1, keepdims=True)
        first = jnp.min(jnp.where(work == m, lane, LANE), axis=-1, keepdims=True)
        pick = lane == first
        chosen = jnp.logical_or(chosen, pick)
        idx = jnp.where(lane == kk, first, idx)
        work = jnp.where(pick, 2.0 * NEG, work)
    sel = jnp.where(chosen, scores, 0.0)
    gate_ref[...] = sel / jnp.sum(sel, axis=-1, keepdims=True) * ROUTED_SCALE
    idx_ref[...] = idx


def router(a2, w_router_pad, bias_pad, E, tm):
    T, D = a2.shape
    return pl.pallas_call(
        functools.partial(_router_kernel, E=E),
        grid=(T // tm,),
        in_specs=[pl.BlockSpec((tm, D), lambda i: (i, 0)),
                  pl.BlockSpec((D, LANE), lambda i: (0, 0)),
                  pl.BlockSpec((1, LANE), lambda i: (0, 0))],
        out_specs=[pl.BlockSpec((tm, LANE), lambda i: (i, 0)), pl.BlockSpec((tm, LANE), lambda i: (i, 0))],
        out_shape=[jax.ShapeDtypeStruct((T, LANE), F32), jax.ShapeDtypeStruct((T, LANE), jnp.int32)],
        compiler_params=_cparams(("parallel",)),
        name="router",
    )(a2, w_router_pad, bias_pad)


def route_plan(idx8, gate, E, tme):
    T = idx8.shape[0]
    P = T * TOP_K
    nt = -(-P // tme) + E
    sorted_key = jnp.sort(idx8.reshape(P) * P + jnp.arange(P, dtype=jnp.int32))
    sorted_e = sorted_key // P
    sorted_p = sorted_key % P
    starts = jnp.searchsorted(sorted_e, jnp.arange(E + 1, dtype=jnp.int32), side="left").astype(jnp.int32)
    counts = starts[1:] - starts[:-1]
    tiles_e = (counts + tme - 1) // tme
    tile_cum = jnp.cumsum(tiles_e)
    n_used = tile_cum[-1:].astype(jnp.int32)
    tile_ids = jnp.arange(nt, dtype=jnp.int32)
    te = jnp.minimum(jnp.searchsorted(tile_cum, tile_ids, side="right"), E - 1).astype(jnp.int32)
    local = tile_ids - (tile_cum - tiles_e)[te]
    row_in_e = local[:, None] * tme + jnp.arange(tme, dtype=jnp.int32)[None, :]
    valid = (row_in_e < counts[te][:, None]) & (tile_ids[:, None] < n_used)
    pair = sorted_p[jnp.clip(starts[te][:, None] + row_in_e, 0, P - 1)]
    tok = jnp.where(valid, pair // TOP_K, 0)
    dummy = P + (tile_ids % 2)[:, None] * tme + jnp.arange(tme, dtype=jnp.int32)[None, :]
    dst = jnp.where(valid, (pair % TOP_K) * T + pair // TOP_K, dummy)
    wts = jnp.take_along_axis(gate, idx8, axis=1).reshape(P)
    w = jnp.where(valid, wts[pair], 0.0).reshape(nt * tme, 1)
    return te, n_used, tok.reshape(nt, 1, tme), dst.reshape(nt, 1, tme), w


def _experts_kernel(te_ref, nu_ref, tok_ref, tokn_ref, dst_ref, w_ref, wgu_ref, wd_ref, x_hbm, y_hbm,
                    xbuf, ybuf, gsem, ssem, *, DE, tme):
    i = pl.program_id(0)
    nu = nu_ref[0]
    slot = i % 2

    def row_in(t, s, r):
        return pltpu.make_async_copy(x_hbm.at[pl.ds(t, 1)], xbuf.at[s, pl.ds(r, 1)], gsem.at[s])

    def row_out(t, s, r):
        return pltpu.make_async_copy(ybuf.at[s, pl.ds(r, 1)], y_hbm.at[pl.ds(t, 1)], ssem.at[s])

    def gather(tref, s):
        def body(r, carry):
            row_in(tref[0, 0, r], s, r).start()
            return carry
        lax.fori_loop(0, tme, body, 0, unroll=8)

    def wait_in(s):
        pltpu.make_async_copy(x_hbm.at[pl.ds(0, tme)], xbuf.at[s], gsem.at[s]).wait()

    def wait_out(s):
        pltpu.make_async_copy(ybuf.at[s], y_hbm.at[pl.ds(0, tme)], ssem.at[s]).wait()

    @pl.when(i == 0)
    def _():
        gather(tok_ref, 0)

    @pl.when(i + 1 < nu)
    def _():
        gather(tokn_ref, 1 - slot)

    @pl.when(i < nu)
    def _():
        wait_in(slot)

        @pl.when(i >= 2)
        def _():
            wait_out(slot)

        lo, hi = _unpack_rows(xbuf[slot])
        x16 = jnp.concatenate([lo, hi], axis=1).astype(BF16)
        gu = _dot(x16, wgu_ref[0])
        hid = (_silu(gu[:, :DE]) * gu[:, DE:] * w_ref[...]).astype(BF16)
        ybuf[slot] = _pack_rows(_dot(hid, wd_ref[0]))

        def body(r, carry):
            row_out(dst_ref[0, 0, r], slot, r).start()
            return carry
        lax.fori_loop(0, tme, body, 0, unroll=8)

    @pl.when(i == nu - 1)
    def _():
        wait_out(slot)

        @pl.when(i >= 1)
        def _():
            wait_out(1 - slot)


def experts_routed(a2p, plan, wgu, wd, tme):
    te, n_used, tok, dst, w = plan
    T, dh = a2p.shape
    E, D, de2 = wgu.shape
    nt = tok.shape[0]
    P = T * TOP_K
    smem_tile = lambda m: pl.BlockSpec((1, 1, tme), m, memory_space=pltpu.SMEM)
    grid_spec = pltpu.PrefetchScalarGridSpec(
        num_scalar_prefetch=2,
        grid=(nt,),
        in_specs=[smem_tile(lambda i, te, nu: (i, 0, 0)),
                  smem_tile(lambda i, te, nu: (jnp.minimum(i + 1, nt - 1), 0, 0)),
                  smem_tile(lambda i, te, nu: (i, 0, 0)),
                  pl.BlockSpec((tme, 1), lambda i, te, nu: (i, 0)),
                  pl.BlockSpec((1, D, de2), lambda i, te, nu: (te[i], 0, 0)),
                  pl.BlockSpec((1, de2 // 2, D), lambda i, te, nu: (te[i], 0, 0)),
                  pl.BlockSpec(memory_space=pl.ANY)],
        out_specs=pl.BlockSpec(memory_space=pl.ANY),
        scratch_shapes=[pltpu.VMEM((2, tme, dh), jnp.uint32), pltpu.VMEM((2, tme, dh), jnp.uint32),
                        pltpu.SemaphoreType.DMA((2,)), pltpu.SemaphoreType.DMA((2,))])
    return pl.pallas_call(
        functools.partial(_experts_kernel, DE=de2 // 2, tme=tme),
        grid_spec=grid_spec,
        out_shape=jax.ShapeDtypeStruct((P + T, dh), jnp.uint32),
        compiler_params=_cparams(("arbitrary",)),
        name="experts_routed",
    )(te, n_used, tok, tok, dst, w, wgu, wd, a2p)


def _shared_ln_kernel(a_ref, f_ref, x_ref, wgu_ref, wd_ref, g_ref, b_ref, gl_ref, gc_ref, shl_ref, shc_ref,
                      scl_ref, scc_ref, x2_ref, a1_ref, *, S, L, alpha, DS):
    tm = x_ref.shape[0]
    ctx = _is_ctx_rows(pl.program_id(0), tm, S, L)
    gu = _dot(a_ref[...].astype(BF16), wgu_ref[...])
    f = _dot((_silu(gu[:, :DS]) * gu[:, DS:]).astype(BF16), wd_ref[...])
    lo, hi = _unpack_rows(f_ref[0])
    for k in range(1, TOP_K):
        lo_k, hi_k = _unpack_rows(f_ref[k])
        lo, hi = lo + lo_k, hi + hi_k
    f = f + jnp.concatenate([lo, hi], axis=1)
    gate = jnp.where(ctx, gc_ref[0, 0], gl_ref[0, 0])
    x2 = _layer_norm(alpha * x_ref[...] + gate * f, g_ref[...], b_ref[...])
    x2_ref[...] = x2
    sh = jnp.where(ctx, shc_ref[0, 0], shl_ref[0, 0])
    sc = jnp.where(ctx, scc_ref[0, 0], scl_ref[0, 0])
    a1_ref[...] = (x2 * (1.0 + sc) + sh).astype(a1_ref.dtype)


def shared_ln(a2, f_routed, x1, wgu_s, wd_s, ln_g, ln_b, mod4, mod4_next, B, S, L, alpha, tm):
    T, D = x1.shape
    ds2 = wgu_s.shape[1]
    row = pl.BlockSpec((tm, D), lambda i: (i, 0))
    vec = pl.BlockSpec((1, D), lambda i: (0, 0))
    return pl.pallas_call(
        functools.partial(_shared_ln_kernel, S=S, L=L, alpha=alpha, DS=ds2 // 2),
        grid=(T // tm,),
        in_specs=[row, pl.BlockSpec((TOP_K, tm, D // 2), lambda i: (0, i, 0)), row,
                  pl.BlockSpec((D, ds2), lambda i: (0, 0)),
                  pl.BlockSpec((ds2 // 2, D), lambda i: (0, 0)),
                  vec, vec] + _mod_specs((5,), tm, S, B, D) + _mod_specs((0, 1), tm, S, B, D),
        out_specs=[row, row],
        out_shape=[jax.ShapeDtypeStruct((T, D), F32), jax.ShapeDtypeStruct((T, D), BF16)],
        compiler_params=_cparams(("parallel",)),
        name="shared_ln",
    )(a2, f_routed, x1, wgu_s, wd_s, ln_g, ln_b, mod4, mod4, *([mod4_next] * 4))


def _col_layout(MIX_W, D, HC):
    kv = (MIX_W, MIX_W, HB_KV * DB, HB_KV * DB, MIX_W, MIX_W, MIX_W, 2 * HC, 2 * HC)
    qs = (MIX_W, MIX_W, MIX_W, 3 * D)
    names = ("ak", "av", "bk", "bv", "cq", "ck", "cv", "cb", "ca", "qa", "qb", "z", "gates")
    offs, o = {}, 0
    for n, s in zip(names, kv + qs):
        offs[n] = (o, o + s)
        o += s
    return offs


def kernel(x, c, ctx, c_ctx, w_ada, b_ada, w_in, lam_q1, lam_k1, lam_q2, lam_k2, diff_norm_w, sink, conv_w, a_log, dt_bias, delta_norm_w, w_branch, w_out, ln1_g, ln1_b, ln2_g, ln2_b, w_router, router_bias, w_gate_e, w_up_e, w_down_e, w_gate_s, w_up_s, w_down_s):
    B, L, D = x.shape
    C = ctx.shape[1]
    S = L + C
    T = B * S
    depth = w_ada.shape[0]
    MIX_W = D // 2
    HA = MIX_W // (2 * DA)
    HB = MIX_W // DB
    HC = MIX_W // DKC
    E = w_router.shape[-1]
    alpha = (2 * depth) ** 0.25
    assert 2 * HC <= LANE and E <= LANE

    tm_tok = math.gcd(256, math.gcd(L, C))
    tm_mm = S // 3 if S % 3 == 0 and (S // 3) % 16 == 0 else tm_tok
    tn_mm = min(512, MIX_W)
    tq = math.gcd(256, math.gcd(L, C))
    tc = min(512, MIX_W)
    rc = math.gcd(256, math.gcd(L, C))
    tme = 256

    mb = -(-(B + 1) // 8) * 8
    c_all = jnp.zeros((mb, D), F32).at[:B].set(c).at[B].set(c_ctx)
    mods = compute_mods(c_all, w_ada, b_ada)
    mod4 = [mods[l].reshape(mb, 6, 1, D) for l in range(depth)]

    cos_a, sin_a, qt_a = rope_tables(L, C, DA, tn_mm)
    cos_b, sin_b, qt_b = rope_tables(L, C, DB, tn_mm if (HB + HB_KV) * DB % tn_mm == 0 else DB * HB_KV)
    tn_b = cos_b.shape[1]

    offs = _col_layout(MIX_W, D, HC)
    col = lambda w, n: w[:, offs[n][0]:offs[n][1]]
    n_rest = 5 * MIX_W + 3 * D + HB_KV * DB
    n_rest_pad = -(-n_rest // tn_mm) * tn_mm
    z_blk = 4
    gate_off = 5 * MIX_W
    bv_off = 5 * MIX_W + 3 * D

    h = jnp.concatenate([x, ctx], axis=1).reshape(T, D)
    a1 = modulate(h, mod4[0], B, S, L, tm_tok)

    for l in range(depth):
        wl = w_in[l]
        w_qk_a = jnp.concatenate([col(wl, "qa"), col(wl, "ak")], axis=1).astype(BF16)
        w_qk_b = jnp.concatenate([col(wl, "qb"), col(wl, "bk")], axis=1).astype(BF16)
        w_rest = jnp.concatenate([col(wl, n) for n in ("av", "cq", "ck", "cv", "z", "gates", "bv")]
                                 + [jnp.zeros((D, n_rest_pad - n_rest), F32)], axis=1).astype(BF16)
        w_ba = jnp.concatenate([col(wl, "cb"), col(wl, "ca"), jnp.zeros((D, LANE - 4 * HC), F32)],
                               axis=1).astype(BF16)

        qk_a = token_matmul(a1, w_qk_a, tm_mm, tn_mm, BF16, rope=(cos_a, sin_a, qt_a), S=S)
        qk_b = token_matmul(a1, w_qk_b, tm_mm, tn_b, BF16, rope=(cos_b, sin_b, qt_b), S=S)
        rest = token_matmul(a1, w_rest, tm_mm, tn_mm, BF16)
        ba = token_matmul(a1, w_ba, tm_mm, LANE, F32)

        lam_init = 0.8 - 0.6 * math.exp(-0.3 * l)
        lam = (jnp.exp(jnp.sum(lam_q1[l] * lam_k1[l])) - jnp.exp(jnp.sum(lam_q2[l] * lam_k2[l]))
               + lam_init).reshape(1).astype(F32)
        nw_a = (diff_norm_w[l] * (1.0 - lam_init)).reshape(1, 2 * DA)
        o_a = diff_attention(qk_a.reshape(B, S, -1), rest.reshape(B, S, -1), lam, nw_a, B, S, L, HA, tq)

        o_b = swa_attention(qk_b.reshape(B, S, -1), rest.reshape(B, S, -1), sink[l], B, S, L, C, HB,
                            bv_off // DB)

        qc, kc, vc = delta_prep(rest.reshape(B, S, -1), conv_w[l], B, S, L, MIX_W, MIX_W // tc, tc, rc)
        ba3 = ba.reshape(B, S, LANE)
        cb = ba3[..., :2 * HC].reshape(B, S, 2, HC)
        ca = ba3[..., 2 * HC:4 * HC].reshape(B, S, 2, HC)
        bdir = jnp.moveaxis(jnp.concatenate([cb, ca], axis=-1), 2, 0)
        bac = jnp.pad(bdir, ((0, 0), (0, 0), (0, 0), (0, LANE - 2 * HC)))
        bar = jnp.swapaxes(bdir.reshape(2, B, S // CHUNK, CHUNK, 2 * HC), -1, -2)
        pvec = jnp.stack([a_log[l], dt_bias[l]], axis=1).astype(F32)
        pc = jnp.zeros((2, 8, LANE), F32).at[:, :2, HC:2 * HC].set(pvec)
        pr = jnp.zeros((2, 2 * HC, LANE), F32).at[:, HC:, :2].set(jnp.swapaxes(pvec, 1, 2))
        o_cf, o_cb = delta_scan(qc, kc, vc, bac, bar, pc, pr, B, S, L, HC)
        o_c = delta_post(o_cf.reshape(T, MIX_W), o_cb.reshape(T, MIX_W), rest, delta_norm_w[l].reshape(1, DKC), T, MIX_W, z_blk,
                         tm_tok)

        merged = merge_branches(o_a.reshape(T, MIX_W), o_b.reshape(T, MIX_W), o_c, w_branch[l].astype(BF16),
                                rest, T, D, MIX_W, gate_off // tn_mm, tm_mm, tn_mm)
        x1, a2, a2p = outproj_ln(merged, w_out[l].astype(BF16), h, ln1_g[l].reshape(1, D), ln1_b[l].reshape(1, D),
                            mod4[l], B, S, L, alpha, tm_tok)

        w_r = jnp.pad(w_router[l], ((0, 0), (0, LANE - E)))
        b_r = jnp.pad(router_bias[l], (0, LANE - E)).reshape(1, LANE)
        gate, idx = router(a2, w_r, b_r, E, tm_tok)
        wgu = jnp.concatenate([w_gate_e[l], w_up_e[l]], axis=-1).astype(BF16)
        plan = route_plan(idx[:, :TOP_K], gate, E, tme)
        f_routed = experts_routed(a2p, plan, wgu, w_down_e[l].astype(BF16), tme).reshape(TOP_K + 1, T, D // 2)
        wgu_s = jnp.concatenate([w_gate_s[l], w_up_s[l]], axis=-1).astype(BF16)
        h, a1 = shared_ln(a2, f_routed, x1, wgu_s, w_down_s[l].astype(BF16), ln2_g[l].reshape(1, D),
                          ln2_b[l].reshape(1, D), mod4[l], mod4[min(l + 1, depth - 1)], B, S, L, alpha, tm_tok)

    return h.reshape(B, S, D)[:, :L]
```

```python
import functools
import math

import jax
import jax.numpy as jnp
from jax import lax
from jax.experimental import pallas as pl
from jax.experimental.pallas import tpu as pltpu

F32 = jnp.float32
BF16 = jnp.bfloat16
HIGHEST = lax.Precision.HIGHEST

LANE = 128
DA = 64
DB = 128
HB_KV = 2
DKC = 128
CONV_K = 5
CHUNK = 64
BLOCK = 128
WINDOW = 128
GRID_W = 64
ROPE_BASE = 10000.0
TOP_K = 8
ROUTED_SCALE = 2.5
LN_EPS = 1e-5
RMS_EPS = 1e-6
NEG = -1e30
VMEM_LIMIT = 56 * 1024 * 1024


def _cparams(sem):
    return pltpu.CompilerParams(dimension_semantics=sem, vmem_limit_bytes=VMEM_LIMIT)


def _dot(a, b, precision=None):
    return jnp.dot(a, b, preferred_element_type=F32, precision=precision)


def _dot_nt(a, b, precision=None):
    return lax.dot_general(a, b, (((1,), (1,)), ((), ())), preferred_element_type=F32, precision=precision)


def _dot_tn(a, b, precision=None):
    return lax.dot_general(a, b, (((0,), (0,)), ((), ())), preferred_element_type=F32, precision=precision)


def _silu(x):
    return x * jax.nn.sigmoid(x)


def _softplus(x):
    return jnp.maximum(x, 0.0) + jnp.log1p(jnp.exp(-jnp.abs(x)))


def _layer_norm(y, g, b):
    mu = jnp.mean(y, axis=-1, keepdims=True)
    yc = y - mu
    var = jnp.mean(yc * yc, axis=-1, keepdims=True)
    return yc * lax.rsqrt(var + LN_EPS) * g + b


def _pick(tile, S, L):
    assert S % tile == 0, (S, tile)
    return tile


def _is_ctx_rows(i, tm, S, L):
    row = (i * tm) % S + lax.broadcasted_iota(jnp.int32, (tm, 1), 0)
    return row >= L


def _mods_kernel(c_ref, w_ref, b_ref, o_ref):
    o_ref[0] = _dot(_silu(c_ref[...]), w_ref[0], HIGHEST) + b_ref[0]


def compute_mods(c_all, w_ada, b_ada):
    depth, D, n6 = w_ada.shape
    mb = c_all.shape[0]
    tn = 1024
    return pl.pallas_call(
        _mods_kernel,
        grid=(depth, n6 // tn),
        in_specs=[pl.BlockSpec((mb, D), lambda l, j: (0, 0)),
                  pl.BlockSpec((1, D, tn), lambda l, j: (l, 0, j)),
                  pl.BlockSpec((1, 1, tn), lambda l, j: (l, 0, j))],
        out_specs=pl.BlockSpec((1, mb, tn), lambda l, j: (l, 0, j)),
        out_shape=jax.ShapeDtypeStruct((depth, mb, n6), F32),
        compiler_params=_cparams(("arbitrary", "arbitrary")),
        name="mods",
    )(c_all, w_ada, b_ada.reshape(depth, 1, n6))


def _mod_specs(parts, tm, S, B, D):
    specs = []
    for p in parts:
        specs.append(pl.BlockSpec((1, 1, 1, D), lambda i, p=p: ((i * tm) // S, p, 0, 0)))
        specs.append(pl.BlockSpec((1, 1, 1, D), lambda i, p=p: (B, p, 0, 0)))
    return specs


def _modulate_kernel(x_ref, shl_ref, shc_ref, scl_ref, scc_ref, a_ref, *, S, L):
    tm = x_ref.shape[0]
    ctx = _is_ctx_rows(pl.program_id(0), tm, S, L)
    sh = jnp.where(ctx, shc_ref[0, 0], shl_ref[0, 0])
    sc = jnp.where(ctx, scc_ref[0, 0], scl_ref[0, 0])
    a_ref[...] = (x_ref[...] * (1.0 + sc) + sh).astype(a_ref.dtype)


def modulate(x, mod4, B, S, L, tm):
    T, D = x.shape
    return pl.pallas_call(
        functools.partial(_modulate_kernel, S=S, L=L),
        grid=(T // tm,),
        in_specs=[pl.BlockSpec((tm, D), lambda i: (i, 0))] + _mod_specs((0, 1), tm, S, B, D),
        out_specs=pl.BlockSpec((tm, D), lambda i: (i, 0)),
        out_shape=jax.ShapeDtypeStruct((T, D), BF16),
        compiler_params=_cparams(("parallel",)),
        name="modulate",
    )(x, mod4, mod4, mod4, mod4)


def _mm_kernel(a_ref, w_ref, *rest, quarter):
    o_ref = rest[-1]
    acc = _dot(a_ref[...], w_ref[...])
    if quarter:
        cos_ref, sin_ref = rest[0], rest[1]
        tn = acc.shape[1]
        lane = lax.broadcasted_iota(jnp.int32, acc.shape, 1)
        first = (lane % (2 * quarter)) < quarter
        rot = jnp.where(first, pltpu.roll(acc, tn - quarter, 1), pltpu.roll(acc, quarter, 1))
        acc = acc * cos_ref[...] + rot * sin_ref[...]
    o_ref[...] = acc.astype(o_ref.dtype)


def token_matmul(a, w, tm, tn, out_dtype, rope=None, S=None):
    T, K = a.shape
    N = w.shape[1]
    assert T % tm == 0 and N % tn == 0
    in_specs = [pl.BlockSpec((tm, K), lambda i, j: (i, 0)),
                pl.BlockSpec((K, tn), lambda i, j: (0, j))]
    args = [a, w]
    quarter = 0
    if rope is not None:
        cos, sin, quarter = rope
        nper = S // tm
        in_specs += [pl.BlockSpec((tm, tn), lambda i, j: (i % nper, 0)),
                     pl.BlockSpec((tm, tn), lambda i, j: (i % nper, 0))]
        args += [cos, sin]
    return pl.pallas_call(
        functools.partial(_mm_kernel, quarter=quarter),
        grid=(T // tm, N // tn),
        in_specs=in_specs,
        out_specs=pl.BlockSpec((tm, tn), lambda i, j: (i, j)),
        out_shape=jax.ShapeDtypeStruct((T, N), out_dtype),
        compiler_params=_cparams(("parallel", "arbitrary")),
        name="token_matmul",
    )(*args)


def rope_tables(L, C, d, tn):
    half, qt = d // 2, d // 4
    rows = L // GRID_W
    row = jnp.repeat(jnp.arange(rows), GRID_W)
    col = jnp.tile(jnp.arange(GRID_W), rows)
    inv = jnp.power(ROPE_BASE, -jnp.arange(0, half, 2, dtype=F32) / half)
    ang_r = row.astype(F32)[:, None] * inv[None, :]
    ang_c = col.astype(F32)[:, None] * inv[None, :]
    ang = jnp.concatenate([ang_r, ang_r, ang_c, ang_c], axis=-1)
    sign = jnp.where((jnp.arange(d) % half) < qt, -1.0, 1.0).astype(F32)
    cos = jnp.concatenate([jnp.cos(ang), jnp.ones((C, d), F32)], axis=0)
    sin = jnp.concatenate([jnp.sin(ang) * sign[None, :], jnp.zeros((C, d), F32)], axis=0)
    return jnp.tile(cos, (1, tn // d)), jnp.tile(sin, (1, tn // d)), qt


def _diff_attn_kernel(lam_ref, q_ref, k_ref, v_ref, nw_ref, o_ref, *, L, nlat):
    qi = pl.program_id(2)
    q = q_ref[0] * (DA ** -0.5)
    lane = lax.broadcasted_iota(jnp.int32, q.shape, 1)
    q1 = jnp.where(lane < DA, q, jnp.zeros_like(q))
    q2 = jnp.where(lane >= DA, q, jnp.zeros_like(q))

    k = k_ref[0]
    v = v_ref[0]
    tq, S = q.shape[0], k.shape[0]

    def attend(valid):
        def probs(qm):
            s = _dot_nt(qm, k)
            if valid is not None:
                s = jnp.where(valid, s, NEG)
            e = jnp.exp(s - jnp.max(s, axis=-1, keepdims=True))
            return e * (1.0 / jnp.sum(e, axis=-1, keepdims=True))

        p = (probs(q1) - lam_ref[0] * probs(q2)).astype(v.dtype)
        o = _dot(p, v)
        o = o * lax.rsqrt(jnp.mean(o * o, axis=-1, keepdims=True) + RMS_EPS) * nw_ref[...]
        o_ref[0] = o.astype(o_ref.dtype)

    @pl.when(qi < nlat)
    def _():
        attend(None)

    @pl.when(qi >= nlat)
    def _():
        row = qi * tq + lax.broadcasted_iota(jnp.int32, (tq, 1), 0)
        kidx = lax.broadcasted_iota(jnp.int32, (tq, S), 1)
        attend((kidx >= L) | (row < L))


def diff_attention(qk_a, rest, lam, norm_w_scaled, B, S, L, HA, tq):
    nlat = L // tq
    return pl.pallas_call(
        functools.partial(_diff_attn_kernel, L=L, nlat=nlat),
        grid=(B, HA, S // tq),
        in_specs=[pl.BlockSpec(memory_space=pltpu.SMEM),
                  pl.BlockSpec((1, tq, LANE), lambda b, h, i: (b, i, h)),
                  pl.BlockSpec((1, S, LANE), lambda b, h, i: (b, 0, HA + h)),
                  pl.BlockSpec((1, S, LANE), lambda b, h, i: (b, 0, h)),
                  pl.BlockSpec((1, LANE), lambda b, h, i: (0, 0))],
        out_specs=pl.BlockSpec((1, tq, LANE), lambda b, h, i: (b, i, h)),
        out_shape=jax.ShapeDtypeStruct((B, S, HA * LANE), BF16),
        compiler_params=_cparams(("parallel", "parallel", "arbitrary")),
        name="diff_attention",
    )(lam, qk_a, qk_a, rest, norm_w_scaled)


def _swa_kernel(sink_ref, q_ref, kp_ref, kc_ref, kn_ref, kx_ref, vp_ref, vc_ref, vn_ref, vx_ref, o_ref,
                *, L, nb, G):
    h = pl.program_id(1)
    n = pl.program_id(2)
    q = jnp.concatenate([q_ref[0, :, g * DB:(g + 1) * DB] for g in range(G)], axis=0)
    k_loc = jnp.concatenate([kp_ref[0], kc_ref[0], kn_ref[0]], axis=0)
    v_loc = jnp.concatenate([vp_ref[0], vc_ref[0], vn_ref[0]], axis=0)
    rows = G * BLOCK
    scale = DB ** -0.5
    s_loc = _dot_nt(q, k_loc) * scale
    s_ctx = _dot_nt(q, kx_ref[0]) * scale
    r = lax.broadcasted_iota(jnp.int32, (rows, 3 * BLOCK), 0)
    j = lax.broadcasted_iota(jnp.int32, (rows, 3 * BLOCK), 1)
    qpos = n * BLOCK + r % BLOCK
    kpos = (n - 1) * BLOCK + j
    rel = kpos - qpos
    ok = (rel <= WINDOW) & (rel >= -WINDOW) & (kpos >= 0) & (kpos < L) & (n < nb)
    s_loc = jnp.where(ok, s_loc, NEG)
    g_of_row = lax.broadcasted_iota(jnp.int32, (rows, 1), 0) // BLOCK
    sink = jnp.zeros((rows, 1), F32)
    for g in range(G):
        sink = jnp.where(g_of_row == g, sink_ref[h * G + g], sink)
    m = jnp.maximum(jnp.maximum(jnp.max(s_loc, axis=-1, keepdims=True),
                                jnp.max(s_ctx, axis=-1, keepdims=True)), sink)
    e_loc = jnp.exp(s_loc - m)
    e_ctx = jnp.exp(s_ctx - m)
    den = (jnp.sum(e_loc, axis=-1, keepdims=True) + jnp.sum(e_ctx, axis=-1, keepdims=True)
           + jnp.exp(sink - m))
    inv = 1.0 / den
    o = _dot((e_loc * inv).astype(v_loc.dtype), v_loc) + _dot((e_ctx * inv).astype(v_loc.dtype), vx_ref[0])
    o_ref[0] = jnp.concatenate([o[g * BLOCK:(g + 1) * BLOCK] for g in range(G)], axis=1).astype(o_ref.dtype)


def swa_attention(qk_b, rest, sink, B, S, L, C, HB, v_blk0):
    G = HB // HB_KV
    nb = L // BLOCK
    xblk = L // C
    assert L % C == 0

    def loc(off, cblk):
        return lambda b, h, n: (b, jnp.clip(jnp.where(n < nb, n + off, 0), 0, nb - 1), cblk + h)

    kspec = lambda off: pl.BlockSpec((1, BLOCK, DB), loc(off, HB))
    vspec = lambda off: pl.BlockSpec((1, BLOCK, DB), loc(off, v_blk0))
    return pl.pallas_call(
        functools.partial(_swa_kernel, L=L, nb=nb, G=G),
        grid=(B, HB_KV, S // BLOCK),
        in_specs=[pl.BlockSpec(memory_space=pltpu.SMEM),
                  pl.BlockSpec((1, BLOCK, G * DB), lambda b, h, n: (b, n, h)),
                  kspec(-1), kspec(0), kspec(1),
                  pl.BlockSpec((1, C, DB), lambda b, h, n: (b, xblk, HB + h)),
                  vspec(-1), vspec(0), vspec(1),
                  pl.BlockSpec((1, C, DB), lambda b, h, n: (b, xblk, v_blk0 + h))],
        out_specs=pl.BlockSpec((1, BLOCK, G * DB), lambda b, h, n: (b, n, h)),
        out_shape=jax.ShapeDtypeStruct((B, S, HB * DB), BF16),
        compiler_params=_cparams(("parallel", "parallel", "arbitrary")),
        name="swa_attention",
    )(sink, qk_b, qk_b, qk_b, qk_b, qk_b, rest, rest, rest, rest)


def _delta_prep_kernel(x_ref, w_ref, o_ref, xs_ref, *, L, rc, mode):
    S, tc = x_ref.shape[1], x_ref.shape[2]
    pad = CONV_K // 2
    halo = 8
    xs_ref[0:halo, :] = jnp.zeros((halo, tc), F32)
    xs_ref[halo + S:halo + S + halo, :] = jnp.zeros((halo, tc), F32)
    xs_ref[halo:halo + S, :] = x_ref[0].astype(F32)
    w = w_ref[...]
    for r0 in range(0, S, rc):
        t = r0 + lax.broadcasted_iota(jnp.int32, (rc, 1), 0)
        acc = jnp.zeros((rc, tc), F32)
        for j in range(CONV_K):
            sl = xs_ref[halo + r0 + j - pad:halo + r0 + j - pad + rc, :]
            src = t + (j - pad)
            if r0 + rc == L and j > pad:
                sl = jnp.where(src < L, sl, 0.0)
            if r0 == L and j < pad:
                sl = jnp.where(src >= L, sl, 0.0)
            acc = acc + sl * w[j:j + 1, :]
        y = _silu(acc)
        if mode != "v":
            parts = []
            for h in range(tc // DKC):
                yh = y[:, h * DKC:(h + 1) * DKC]
                yh = yh * lax.rsqrt(jnp.sum(yh * yh, axis=-1, keepdims=True) + RMS_EPS)
                parts.append(yh * (DKC ** -0.5) if mode == "q" else yh)
            y = jnp.concatenate(parts, axis=1) if len(parts) > 1 else parts[0]
        o_ref[0, r0:r0 + rc, :] = y


def delta_prep(rest, conv_w_l, B, S, L, MIX_W, col_blk0, tc, rc):
    outs = []
    nt = MIX_W // tc
    for pi, mode in enumerate(("q", "k", "v")):
        outs.append(pl.pallas_call(
            functools.partial(_delta_prep_kernel, L=L, rc=rc, mode=mode),
            grid=(B, nt),
            in_specs=[pl.BlockSpec((1, S, tc), lambda b, j, pi=pi: (b, 0, col_blk0 + pi * nt + j)),
                      pl.BlockSpec((CONV_K, tc), lambda b, j, pi=pi: (0, pi * nt + j))],
            out_specs=pl.BlockSpec((1, S, tc), lambda b, j: (b, 0, j)),
            out_shape=jax.ShapeDtypeStruct((B, S, MIX_W), F32),
            scratch_shapes=[pltpu.VMEM((S + 16, tc), F32)],
            compiler_params=_cparams(("parallel", "parallel")),
            name="delta_prep_" + mode,
        )(rest, conv_w_l))
    return outs


def _split16(x):
    hi = x.astype(BF16)
    return hi, (x - hi.astype(F32)).astype(BF16)


def _prod2(x, y):
    return _dot(jnp.concatenate(x, axis=1), jnp.concatenate([y[0], y[1], y[0], y[1]], axis=0))


def _unit_tri_inverse(a_list, ii, jj):
    same16 = (ii // 16) == (jj // 16)
    same32 = (ii // 32) == (jj // 32)
    eye = (ii == jj).astype(F32)
    each = lambda f, *ls: [f(*xs) for xs in zip(*ls)]
    d = each(lambda a: _split16(jnp.where(same16, a, 0.0)), a_list)
    l1 = each(lambda a: _split16(jnp.where(same32 & jnp.logical_not(same16), a, 0.0)), a_list)
    l2 = each(lambda a: _split16(jnp.where(same32, 0.0, a)), a_list)
    d2 = each(lambda x: _split16(_prod2(x, x)), d)
    d4 = each(lambda x: _split16(_prod2(x, x)), d2)
    d8 = each(lambda x: _split16(_prod2(x, x)), d4)
    t = each(lambda a: eye - jnp.where(same16, a, 0.0), a_list)
    for p in (d2, d4, d8):
        t = each(lambda tt, pp: tt + _prod2(_split16(tt), pp), t, p)
    for l in (l1, l2):
        ts = each(_split16, t)
        m = each(lambda tt, ll: _split16(_prod2(tt, ll)), ts, l)
        t = each(lambda tf, mm, tt: tf - _prod2(mm, tt), t, m, ts)
    return t


def _delta_scan_kernel(qf_ref, kf_ref, vf_ref, qb_ref, kb_ref, vb_ref, bacf_ref, bacb_ref, barf_ref, barb_ref,
                       pc_ref, pr_ref, of_ref, ob_ref, s_ref, *, HC):
    @pl.when(pl.program_id(1) == 0)
    def _():
        s_ref[...] = jnp.zeros_like(s_ref)

    ii = lax.broadcasted_iota(jnp.int32, (CHUNK, 2 * CHUNK), 0)
    jj = lax.broadcasted_iota(jnp.int32, (CHUNK, 2 * CHUNK), 1) % CHUNK
    dirs = ((qf_ref, kf_ref, vf_ref, bacf_ref, barf_ref, of_ref), (qb_ref, kb_ref, vb_ref, bacb_ref, barb_ref, ob_ref))
    chains = []
    for dn, (q_ref, k_ref, v_ref, bac_ref, bar_ref, o_ref) in enumerate(dirs):
        ahead = (ii - jj) if dn == 0 else (jj - ii)
        strict = ahead > 0
        incl = ahead >= 0
        tri2 = incl.astype(F32)
        tri = tri2[:, :CHUNK]
        bac = bac_ref[0, 0]
        pc = pc_ref[dn]
        beta_c = jax.nn.sigmoid(bac)
        g_c = -jnp.exp(pc[0:1]) * _softplus(bac + pc[1:2])
        gc_c = _dot(tri, g_c, HIGHEST)
        gtot_c = jnp.sum(g_c, axis=0, keepdims=True)
        bar = bar_ref[0, 0, 0]
        pr = pr_ref[dn]
        g_r = -jnp.exp(pr[:, 0:1]) * _softplus(bar + pr[:, 1:2])
        gc_r = _dot_nt(g_r, jnp.concatenate([tri, tri], axis=0), HIGHEST)
        for h in range(HC):
            chains.append(dict(dn=dn, h=h, q_ref=q_ref, k_ref=k_ref, v_ref=v_ref, o_ref=o_ref, strict=strict,
                               incl=incl, beta=beta_c[:, h:h + 1], gcol=gc_c[:, HC + h:HC + h + 1],
                               grow=gc_r[HC + h:HC + h + 1, :], gtot=gtot_c[:, HC + h:HC + h + 1]))

    sl = lambda ch: slice(ch["h"] * DKC, (ch["h"] + 1) * DKC)
    for ch in chains:
        ch["k"] = ch["k_ref"][0, :, sl(ch)]
        ch["k16"] = ch["k"].astype(BF16)
        ch["kb"] = ch["k"] * ch["beta"]
        ch["dec"] = jnp.exp(jnp.where(ch["incl"], ch["gcol"] - ch["grow"], NEG))
        ch["egc"] = jnp.exp(ch["gcol"])
    for ch in chains:
        k2 = jnp.concatenate([ch["k16"], ch["k16"]], axis=0)
        ch["a"] = _dot_nt(ch["kb"].astype(BF16), k2) * jnp.where(ch["strict"], ch["dec"], 0.0)
    t_list = _unit_tri_inverse([ch["a"] for ch in chains], ii, jj)
    for ch, t in zip(chains, t_list):
        v = ch["v_ref"][0, :, sl(ch)]
        rhs = jnp.concatenate([v * ch["beta"], ch["kb"] * ch["egc"]], axis=1).astype(BF16)
        ch["uw"] = _dot(t[:, :CHUNK].astype(BF16), rhs)
    for ch in chains:
        q = ch["q_ref"][0, :, sl(ch)]
        ch["qk"] = (_dot_nt(q.astype(BF16), ch["k16"]) * ch["dec"][:, :CHUNK]).astype(BF16)
        ch["state"] = s_ref[ch["dn"], ch["h"]]
        lhs = jnp.concatenate([ch["uw"][:, DKC:], q * ch["egc"]], axis=0).astype(BF16)
        ch["ws_qs"] = _dot(lhs, ch["state"].astype(BF16))
    for ch in chains:
        ch["u16"] = (ch["uw"][:, :DKC] - ch["ws_qs"][:CHUNK]).astype(BF16)
    for ch in chains:
        ch["o_ref"][0, :, sl(ch)] = ch["ws_qs"][CHUNK:] + _dot(ch["qk"], ch["u16"])
        k_tail = (ch["k"] * jnp.exp(ch["gtot"] - ch["gcol"])).astype(BF16)
        s_ref[ch["dn"], ch["h"]] = ch["state"] * jnp.exp(ch["gtot"]) + _dot_tn(k_tail, ch["u16"])


def delta_scan(qc, kc, vc, bac, bar, pc, pr, B, S, L, HC):
    nch = S // CHUNK
    nl = L // CHUNK
    fwd = lambda b, c: (b, (c + nl) % nch, 0)
    bwd = lambda b, c: (b, nch - 1 - c, 0)
    mw = HC * DKC
    row = lambda m: pl.BlockSpec((1, CHUNK, mw), m)
    bac_spec = lambda dn, m: pl.BlockSpec((1, 1, CHUNK, LANE), lambda b, c: (dn,) + m(b, c))
    bar_spec = lambda dn, m: pl.BlockSpec((1, 1, 1, 2 * HC, CHUNK), lambda b, c: (dn,) + m(b, c) + (0,))
    return pl.pallas_call(
        functools.partial(_delta_scan_kernel, HC=HC),
        grid=(B, nch),
        in_specs=[row(fwd), row(fwd), row(fwd), row(bwd), row(bwd), row(bwd),
                  bac_spec(0, fwd), bac_spec(1, bwd), bar_spec(0, fwd), bar_spec(1, bwd),
                  pl.BlockSpec((2, 8, LANE), lambda b, c: (0, 0, 0)),
                  pl.BlockSpec((2, 2 * HC, LANE), lambda b, c: (0, 0, 0))],
        out_specs=[row(fwd), row(bwd)],
        out_shape=[jax.ShapeDtypeStruct((B, S, mw), F32), jax.ShapeDtypeStruct((B, S, mw), F32)],
        scratch_shapes=[pltpu.VMEM((2, HC, DKC, DKC), F32)],
        compiler_params=_cparams(("parallel", "arbitrary")),
        name="delta_scan",
    )(qc, kc, vc, qc, kc, vc, bac, bac, bar, bar, pc, pr)


def _delta_post_kernel(of_ref, ob_ref, z_ref, nw_ref, y_ref):
    o = of_ref[...] + ob_ref[...]
    z = z_ref[...].astype(F32)
    nw = nw_ref[...]
    parts = []
    for h in range(o.shape[1] // DKC):
        oh = o[:, h * DKC:(h + 1) * DKC]
        oh = oh * lax.rsqrt(jnp.mean(oh * oh, axis=-1, keepdims=True) + RMS_EPS) * nw
        parts.append(oh * _silu(z[:, h * DKC:(h + 1) * DKC]))
    y_ref[...] = jnp.concatenate(parts, axis=1).astype(y_ref.dtype)


def delta_post(o_f, o_b, rest, norm_w, T, MIX_W, z_blk, tm):
    return pl.pallas_call(
        _delta_post_kernel,
        grid=(T // tm,),
        in_specs=[pl.BlockSpec((tm, MIX_W), lambda i: (i, 0)),
                  pl.BlockSpec((tm, MIX_W), lambda i: (i, 0)),
                  pl.BlockSpec((tm, MIX_W), lambda i: (i, z_blk)),
                  pl.BlockSpec((1, DKC), lambda i: (0, 0))],
        out_specs=pl.BlockSpec((tm, MIX_W), lambda i: (i, 0)),
        out_shape=jax.ShapeDtypeStruct((T, MIX_W), BF16),
        compiler_params=_cparams(("parallel",)),
        name="delta_post",
    )(o_f, o_b, rest, norm_w)


def _merge_kernel(oa_ref, ob_ref, oc_ref, w_ref, ga_ref, gb_ref, gc_ref, m_ref):
    acc = jax.nn.sigmoid(ga_ref[...].astype(F32)) * _dot(oa_ref[...], w_ref[0])
    acc = acc + jax.nn.sigmoid(gb_ref[...].astype(F32)) * _dot(ob_ref[...], w_ref[1])
    acc = acc + jax.nn.sigmoid(gc_ref[...].astype(F32)) * _dot(oc_ref[...], w_ref[2])
    m_ref[...] = acc.astype(m_ref.dtype)


def merge_branches(oa, ob, oc, w_branch, rest, T, D, MIX_W, gate_blk0, tm, tn):
    nd = D // tn
    o_spec = pl.BlockSpec((tm, MIX_W), lambda i, j: (i, 0))
    g_spec = lambda br: pl.BlockSpec((tm, tn), lambda i, j, br=br: (i, gate_blk0 + br * nd + j))
    return pl.pallas_call(
        _merge_kernel,
        grid=(T // tm, nd),
        in_specs=[o_spec, o_spec, o_spec,
                  pl.BlockSpec((3, MIX_W, tn), lambda i, j: (0, 0, j)),
                  g_spec(0), g_spec(1), g_spec(2)],
        out_specs=pl.BlockSpec((tm, tn), lambda i, j: (i, j)),
        out_shape=jax.ShapeDtypeStruct((T, D), BF16),
        compiler_params=_cparams(("parallel", "arbitrary")),
        name="merge_branches",
    )(oa, ob, oc, w_branch, rest, rest, rest)


def _pack_rows(y):
    half = y.shape[1] // 2

    def rne_bits(v):
        b = lax.bitcast_convert_type(v, jnp.uint32)
        return b + jnp.uint32(0x7FFF) + ((b >> 16) & jnp.uint32(1))

    return (rne_bits(y[:, half:]) & jnp.uint32(0xFFFF0000)) | (rne_bits(y[:, :half]) >> 16)


def _unpack_rows(w):
    lo = lax.bitcast_convert_type(w << 16, F32)
    hi = lax.bitcast_convert_type(w & jnp.uint32(0xFFFF0000), F32)
    return lo, hi


def _outproj_ln_kernel(m_ref, w_ref, x_ref, g_ref, b_ref, gl_ref, gc_ref, shl_ref, shc_ref, scl_ref, scc_ref,
                       x1_ref, a2_ref, a2p_ref, *, S, L, alpha):
    tm = x_ref.shape[0]
    ctx = _is_ctx_rows(pl.program_id(0), tm, S, L)
    gate = jnp.where(ctx, gc_ref[0, 0], gl_ref[0, 0])
    y = alpha * x_ref[...] + gate * _dot(m_ref[...], w_ref[...])
    x1 = _layer_norm(y, g_ref[...], b_ref[...])
    x1_ref[...] = x1
    sh = jnp.where(ctx, shc_ref[0, 0], shl_ref[0, 0])
    sc = jnp.where(ctx, scc_ref[0, 0], scl_ref[0, 0])
    a2 = x1 * (1.0 + sc) + sh
    a2_ref[...] = a2
    a2p_ref[...] = _pack_rows(a2)


def outproj_ln(merged, w_out, x, ln_g, ln_b, mod4, B, S, L, alpha, tm):
    T, D = x.shape
    row = pl.BlockSpec((tm, D), lambda i: (i, 0))
    vec = pl.BlockSpec((1, D), lambda i: (0, 0))
    return pl.pallas_call(
        functools.partial(_outproj_ln_kernel, S=S, L=L, alpha=alpha),
        grid=(T // tm,),
        in_specs=[row, pl.BlockSpec((D, D), lambda i: (0, 0)), row, vec, vec] + _mod_specs((2, 3, 4), tm, S, B, D),
        out_specs=[row, row, pl.BlockSpec((tm, D // 2), lambda i: (i, 0))],
        out_shape=[jax.ShapeDtypeStruct((T, D), F32), jax.ShapeDtypeStruct((T, D), F32),
                   jax.ShapeDtypeStruct((T, D // 2), jnp.uint32)],
        compiler_params=_cparams(("parallel",)),
        name="outproj_ln",
    )(merged, w_out, x, ln_g, ln_b, *([mod4] * 6))


def _router_kernel(a_ref, w_ref, bias_ref, gate_ref, idx_ref, *, E):
    a_hi, a_lo = _split16(a_ref[...])
    w_hi, w_lo = _split16(w_ref[...])
    logits = _dot(a_hi, w_hi) + (_dot(a_hi, w_lo) + _dot(a_lo, w_hi))
    scores = jax.nn.sigmoid(logits)
    lane = lax.broadcasted_iota(jnp.int32, scores.shape, 1)
    work = jnp.where(lane < E, scores + bias_ref[...], NEG)
    chosen = jnp.zeros(scores.shape, jnp.bool_)
    idx = jnp.zeros(scores.shape, jnp.int32)
    for kk in range(TOP_K):
        m = jnp.max(work, axis=-1, keepdims=True)
        first = jnp.min(jnp.where(work == m, lane, LANE), axis=-1, keepdims=True)
        pick = lane == first
        chosen = jnp.logical_or(chosen, pick)
        idx = jnp.where(lane == kk, first, idx)
        work = jnp.where(pick, 2.0 * NEG, work)
    sel = jnp.where(chosen, scores, 0.0)
    gate_ref[...] = sel / jnp.sum(sel, axis=-1, keepdims=True) * ROUTED_SCALE
    idx_ref[...] = idx


def router(a2, w_router_pad, bias_pad, E, tm):
    T, D = a2.shape
    return pl.pallas_call(
        functools.partial(_router_kernel, E=E),
        grid=(T // tm,),
        in_specs=[pl.BlockSpec((tm, D), lambda i: (i, 0)),
                  pl.BlockSpec((D, LANE), lambda i: (0, 0)),
                  pl.BlockSpec((1, LANE), lambda i: (0, 0))],
        out_specs=[pl.BlockSpec((tm, LANE), lambda i: (i, 0)), pl.BlockSpec((tm, LANE), lambda i: (i, 0))],
        out_shape=[jax.ShapeDtypeStruct((T, LANE), F32), jax.ShapeDtypeStruct((T, LANE), jnp.int32)],
        compiler_params=_cparams(("parallel",)),
        name="router",
    )(a2, w_router_pad, bias_pad)


def route_plan(idx8, gate, E, tme):
    T = idx8.shape[0]
    P = T * TOP_K
    assert T >= 3 * tme
    nt = -(-P // tme) + E + 1
    e_flat = idx8.reshape(P)
    sorted_p = jnp.sort(e_flat * P + jnp.arange(P, dtype=jnp.int32)) % P
    experts = jnp.arange(E, dtype=jnp.int32)
    counts = jnp.sum((e_flat[:, None] == experts[None, :]).astype(jnp.int32), axis=0)
    starts = jnp.cumsum(counts) - counts
    tiles_e = (counts + tme - 1) // tme
    tile_cum = jnp.cumsum(tiles_e)
    n_used = tile_cum[-1:].astype(jnp.int32)
    tile_ids = jnp.arange(nt, dtype=jnp.int32)
    te = jnp.minimum(jnp.sum((tile_ids[:, None] >= tile_cum[None, :]).astype(jnp.int32), axis=1), E - 1)
    local = tile_ids - (tile_cum - tiles_e)[te]
    lane = jnp.arange(tme, dtype=jnp.int32)[None, :]
    row_in_e = local[:, None] * tme + lane
    valid = (row_in_e < counts[te][:, None]) & (tile_ids[:, None] < n_used)
    pair = sorted_p[jnp.clip(starts[te][:, None] + row_in_e, 0, P - 1)]
    tok = jnp.where(valid, pair // TOP_K, 0)
    dummy = P + jnp.where(tile_ids < n_used, tile_ids % 2, 2)[:, None] * tme + lane
    dst = jnp.where(valid, (pair % TOP_K) * T + pair // TOP_K, dummy)
    wts = jnp.take_along_axis(gate, idx8, axis=1).reshape(P)
    w = jnp.where(valid, wts[pair], 0.0).reshape(nt * tme, 1)
    return te, n_used, tok.reshape(nt, 1, tme), dst.reshape(nt, 1, tme), w


def _experts_kernel(te_ref, nu_ref, tok0_ref, tokn_ref, dstp_ref, w_ref, wgu_ref, wd_ref, x_hbm, y_hbm,
                    xb0, xb1, yb0, yb1, gsem, ssem, *, DE, tme):
    i = pl.program_id(0)
    nu = nu_ref[0]
    xbufs, ybufs = (xb0, xb1), (yb0, yb1)

    def row_in(t, s, r):
        return pltpu.make_async_copy(x_hbm.at[pl.ds(t, 1)], xbufs[s].at[pl.ds(r, 1)], gsem.at[s])

    def row_out(t, s, r):
        return pltpu.make_async_copy(ybufs[s].at[pl.ds(r, 1)], y_hbm.at[pl.ds(t, 1)], ssem.at[s])

    def wait_in(s):
        pltpu.make_async_copy(x_hbm.at[pl.ds(0, tme)], xbufs[s], gsem.at[s]).wait()

    def wait_out(s):
        pltpu.make_async_copy(ybufs[s], y_hbm.at[pl.ds(0, tme)], ssem.at[s]).wait()

    @pl.when(i == 0)
    def _():
        yb1[...] = jnp.zeros_like(yb1)

        def body(r, carry):
            row_in(tok0_ref[0, 0, r], 0, r).start()
            return carry
        lax.fori_loop(0, tme, body, 0, unroll=8)

    def step(s, compute):
        wait_in(s)

        @pl.when(i >= 1)
        def _():
            wait_out(s)

        if compute:
            lo, hi = _unpack_rows(xbufs[s][...])
            x16 = jnp.concatenate([lo, hi], axis=1).astype(BF16)
            gu = _dot(x16, wgu_ref[0])
            hid = (_silu(gu[:, :DE]) * gu[:, DE:] * w_ref[...]).astype(BF16)
            ybufs[s][...] = _pack_rows(_dot(hid, wd_ref[0]))
        for r in range(tme):
            row_out(dstp_ref[0, 0, r], 1 - s, r).start()
            if compute:
                row_in(tokn_ref[0, 0, r], 1 - s, r).start()
        if not compute:
            wait_out(1 - s)

    for s in range(2):
        @pl.when((i < nu) & (i % 2 == s))
        def _(s=s):
            step(s, True)

        @pl.when((i == nu) & (i % 2 == s))
        def _(s=s):
            step(s, False)


def experts_routed(a2p, plan, wgu, wd, tme):
    te, n_used, tok, dst, w = plan
    T, dh = a2p.shape
    E, D, de2 = wgu.shape
    nt = tok.shape[0]
    P = T * TOP_K
    smem_tile = lambda m: pl.BlockSpec((1, 1, tme), m, memory_space=pltpu.SMEM)
    grid_spec = pltpu.PrefetchScalarGridSpec(
        num_scalar_prefetch=2,
        grid=(nt,),
        in_specs=[smem_tile(lambda i, te, nu: (0, 0, 0)),
                  smem_tile(lambda i, te, nu: (jnp.minimum(i + 1, nt - 1), 0, 0)),
                  smem_tile(lambda i, te, nu: (jnp.where(i == 0, nt - 1, i - 1), 0, 0)),
                  pl.BlockSpec((tme, 1), lambda i, te, nu: (i, 0)),
                  pl.BlockSpec((1, D, de2), lambda i, te, nu: (te[i], 0, 0)),
                  pl.BlockSpec((1, de2 // 2, D), lambda i, te, nu: (te[i], 0, 0)),
                  pl.BlockSpec(memory_space=pl.ANY)],
        out_specs=pl.BlockSpec(memory_space=pl.ANY),
        scratch_shapes=[pltpu.VMEM((tme, dh), jnp.uint32), pltpu.VMEM((tme, dh), jnp.uint32),
                        pltpu.VMEM((tme, dh), jnp.uint32), pltpu.VMEM((tme, dh), jnp.uint32),
                        pltpu.SemaphoreType.DMA((2,)), pltpu.SemaphoreType.DMA((2,))])
    return pl.pallas_call(
        functools.partial(_experts_kernel, DE=de2 // 2, tme=tme),
        grid_spec=grid_spec,
        out_shape=jax.ShapeDtypeStruct((P + T, dh), jnp.uint32),
        compiler_params=_cparams(("arbitrary",)),
        name="experts_routed",
    )(te, n_used, tok, tok, dst, w, wgu, wd, a2p)


def _shared_ln_kernel(a_ref, f_ref, x_ref, wgu_ref, wd_ref, g_ref, b_ref, gl_ref, gc_ref, shl_ref, shc_ref,
                      scl_ref, scc_ref, x2_ref, a1_ref, *, S, L, alpha, DS):
    tm = x_ref.shape[0]
    ctx = _is_ctx_rows(pl.program_id(0), tm, S, L)
    gu = _dot(a_ref[...].astype(BF16), wgu_ref[...])
    f = _dot((_silu(gu[:, :DS]) * gu[:, DS:]).astype(BF16), wd_ref[...])
    lo, hi = _unpack_rows(f_ref[0])
    for k in range(1, TOP_K):
        lo_k, hi_k = _unpack_rows(f_ref[k])
        lo, hi = lo + lo_k, hi + hi_k
    f = f + jnp.concatenate([lo, hi], axis=1)
    gate = jnp.where(ctx, gc_ref[0, 0], gl_ref[0, 0])
    x2 = _layer_norm(alpha * x_ref[...] + gate * f, g_ref[...], b_ref[...])
    x2_ref[...] = x2
    sh = jnp.where(ctx, shc_ref[0, 0], shl_ref[0, 0])
    sc = jnp.where(ctx, scc_ref[0, 0], scl_ref[0, 0])
    a1_ref[...] = (x2 * (1.0 + sc) + sh).astype(a1_ref.dtype)


def shared_ln(a2, f_routed, x1, wgu_s, wd_s, ln_g, ln_b, mod4, mod4_next, B, S, L, alpha, tm):
    T, D = x1.shape
    ds2 = wgu_s.shape[1]
    row = pl.BlockSpec((tm, D), lambda i: (i, 0))
    vec = pl.BlockSpec((1, D), lambda i: (0, 0))
    return pl.pallas_call(
        functools.partial(_shared_ln_kernel, S=S, L=L, alpha=alpha, DS=ds2 // 2),
        grid=(T // tm,),
        in_specs=[row, pl.BlockSpec((TOP_K, tm, D // 2), lambda i: (0, i, 0)), row,
                  pl.BlockSpec((D, ds2), lambda i: (0, 0)),
                  pl.BlockSpec((ds2 // 2, D), lambda i: (0, 0)),
                  vec, vec] + _mod_specs((5,), tm, S, B, D) + _mod_specs((0, 1), tm, S, B, D),
        out_specs=[row, row],
        out_shape=[jax.ShapeDtypeStruct((T, D), F32), jax.ShapeDtypeStruct((T, D), BF16)],
        compiler_params=_cparams(("parallel",)),
        name="shared_ln",
    )(a2, f_routed, x1, wgu_s, wd_s, ln_g, ln_b, mod4, mod4, *([mod4_next] * 4))


def _col_layout(MIX_W, D, HC):
    kv = (MIX_W, MIX_W, HB_KV * DB, HB_KV * DB, MIX_W, MIX_W, MIX_W, 2 * HC, 2 * HC)
    qs = (MIX_W, MIX_W, MIX_W, 3 * D)
    names = ("ak", "av", "bk", "bv", "cq", "ck", "cv", "cb", "ca", "qa", "qb", "z", "gates")
    offs, o = {}, 0
    for n, s in zip(names, kv + qs):
        offs[n] = (o, o + s)
        o += s
    return offs


def kernel(x, c, ctx, c_ctx, w_ada, b_ada, w_in, lam_q1, lam_k1, lam_q2, lam_k2, diff_norm_w, sink, conv_w, a_log, dt_bias, delta_norm_w, w_branch, w_out, ln1_g, ln1_b, ln2_g, ln2_b, w_router, router_bias, w_gate_e, w_up_e, w_down_e, w_gate_s, w_up_s, w_down_s):
    B, L, D = x.shape
    C = ctx.shape[1]
    S = L + C
    T = B * S
    depth = w_ada.shape[0]
    MIX_W = D // 2
    HA = MIX_W // (2 * DA)
    HB = MIX_W // DB
    HC = MIX_W // DKC
    E = w_router.shape[-1]
    alpha = (2 * depth) ** 0.25
    assert 2 * HC <= LANE and E <= LANE

    tm_tok = math.gcd(256, math.gcd(L, C))
    tm_mm = S // 3 if S % 3 == 0 and (S // 3) % 16 == 0 else tm_tok
    tn_mm = min(512, MIX_W)
    tq = tm_mm
    tc = min(512, MIX_W)
    rc = math.gcd(256, math.gcd(L, C))
    tme = 256

    mb = -(-(B + 1) // 8) * 8
    c_all = jnp.zeros((mb, D), F32).at[:B].set(c).at[B].set(c_ctx)
    mods = compute_mods(c_all, w_ada, b_ada)
    mod4 = [mods[l].reshape(mb, 6, 1, D) for l in range(depth)]

    cos_a, sin_a, qt_a = rope_tables(L, C, DA, tn_mm)
    cos_b, sin_b, qt_b = rope_tables(L, C, DB, tn_mm if (HB + HB_KV) * DB % tn_mm == 0 else DB * HB_KV)
    tn_b = cos_b.shape[1]

    offs = _col_layout(MIX_W, D, HC)
    col = lambda w, n: w[:, offs[n][0]:offs[n][1]]
    n_rest = 5 * MIX_W + 3 * D + HB_KV * DB
    n_rest_pad = -(-n_rest // tn_mm) * tn_mm
    z_blk = 4
    gate_off = 5 * MIX_W
    bv_off = 5 * MIX_W + 3 * D

    h = jnp.concatenate([x, ctx], axis=1).reshape(T, D)
    a1 = modulate(h, mod4[0], B, S, L, tm_tok)

    for l in range(depth):
        wl = w_in[l].astype(BF16)
        w_qk_a = jnp.concatenate([col(wl, "qa"), col(wl, "ak")], axis=1)
        w_qk_b = jnp.concatenate([col(wl, "qb"), col(wl, "bk")], axis=1)
        w_rest = jnp.concatenate([col(wl, n) for n in ("av", "cq", "ck", "cv", "z", "gates", "bv")]
                                 + [jnp.zeros((D, n_rest_pad - n_rest), BF16)], axis=1)
        w_ba = jnp.concatenate([col(wl, "cb"), col(wl, "ca"), jnp.zeros((D, LANE - 4 * HC), BF16)], axis=1)

        qk_a = token_matmul(a1, w_qk_a, tm_mm, tn_mm, BF16, rope=(cos_a, sin_a, qt_a), S=S)
        qk_b = token_matmul(a1, w_qk_b, tm_mm, tn_b, BF16, rope=(cos_b, sin_b, qt_b), S=S)
        rest = token_matmul(a1, w_rest, tm_mm, tn_mm, BF16)
        ba = token_matmul(a1, w_ba, tm_mm, LANE, F32)

        lam_init = 0.8 - 0.6 * math.exp(-0.3 * l)
        lam = (jnp.exp(jnp.sum(lam_q1[l] * lam_k1[l])) - jnp.exp(jnp.sum(lam_q2[l] * lam_k2[l]))
               + lam_init).reshape(1).astype(F32)
        nw_a = (diff_norm_w[l] * (1.0 - lam_init)).reshape(1, 2 * DA)
        o_a = diff_attention(qk_a.reshape(B, S, -1), rest.reshape(B, S, -1), lam, nw_a, B, S, L, HA, tq)

        o_b = swa_attention(qk_b.reshape(B, S, -1), rest.reshape(B, S, -1), sink[l], B, S, L, C, HB,
                            bv_off // DB)

        qc, kc, vc = delta_prep(rest.reshape(B, S, -1), conv_w[l], B, S, L, MIX_W, MIX_W // tc, tc, rc)
        ba3 = ba.reshape(B, S, LANE)
        cb = ba3[..., :2 * HC].reshape(B, S, 2, HC)
        ca = ba3[..., 2 * HC:4 * HC].reshape(B, S, 2, HC)
        bdir = jnp.moveaxis(jnp.concatenate([cb, ca], axis=-1), 2, 0)
        bac = jnp.pad(bdir, ((0, 0), (0, 0), (0, 0), (0, LANE - 2 * HC)))
        bar = jnp.swapaxes(bdir.reshape(2, B, S // CHUNK, CHUNK, 2 * HC), -1, -2)
        pvec = jnp.stack([a_log[l], dt_bias[l]], axis=1).astype(F32)
        pc = jnp.zeros((2, 8, LANE), F32).at[:, :2, HC:2 * HC].set(pvec)
        pr = jnp.zeros((2, 2 * HC, LANE), F32).at[:, HC:, :2].set(jnp.swapaxes(pvec, 1, 2))
        o_cf, o_cb = delta_scan(qc, kc, vc, bac, bar, pc, pr, B, S, L, HC)
        o_c = delta_post(o_cf.reshape(T, MIX_W), o_cb.reshape(T, MIX_W), rest, delta_norm_w[l].reshape(1, DKC), T, MIX_W, z_blk,
                         tm_tok)

        merged = merge_branches(o_a.reshape(T, MIX_W), o_b.reshape(T, MIX_W), o_c, w_branch[l].astype(BF16),
                                rest, T, D, MIX_W, gate_off // tn_mm, tm_mm, tn_mm)
        x1, a2, a2p = outproj_ln(merged, w_out[l].astype(BF16), h, ln1_g[l].reshape(1, D), ln1_b[l].reshape(1, D),
                            mod4[l], B, S, L, alpha, tm_tok)

        w_r = jnp.pad(w_router[l], ((0, 0), (0, LANE - E)))
        b_r = jnp.pad(router_bias[l], (0, LANE - E)).reshape(1, LANE)
        gate, idx = router(a2, w_r, b_r, E, tm_tok)
        wgu = jnp.concatenate([w_gate_e[l], w_up_e[l]], axis=-1).astype(BF16)
        plan = route_plan(idx[:, :TOP_K], gate, E, tme)
        f_routed = experts_routed(a2p, plan, wgu, w_down_e[l].astype(BF16), tme).reshape(TOP_K + 1, T, D // 2)
        wgu_s = jnp.concatenate([w_gate_s[l], w_up_s[l]], axis=-1).astype(BF16)
        h, a1 = shared_ln(a2, f_routed, x1, wgu_s, w_down_s[l].astype(BF16), ln2_g[l].reshape(1, D),
                          ln2_b[l].reshape(1, D), mod4[l], mod4[min(l + 1, depth - 1)], B, S, L, alpha, tm_tok)

    return h.reshape(B, S, D)[:, :L]
```

```python
import functools
import math

import jax
import jax.numpy as jnp
from jax import lax
from jax.experimental import pallas as pl
from jax.experimental.pallas import tpu as pltpu

F32 = jnp.float32
BF16 = jnp.bfloat16
HIGHEST = lax.Precision.HIGHEST

LANE = 128
DA = 64
DB = 128
HB_KV = 2
DKC = 128
CONV_K = 5
CHUNK = 64
BLOCK = 128
WINDOW = 128
GRID_W = 64
ROPE_BASE = 10000.0
TOP_K = 8
ROUTED_SCALE = 2.5
LN_EPS = 1e-5
RMS_EPS = 1e-6
NEG = -1e30
VMEM_LIMIT = 56 * 1024 * 1024


def _cparams(sem):
    return pltpu.CompilerParams(dimension_semantics=sem, vmem_limit_bytes=VMEM_LIMIT)


def _dot(a, b, precision=None):
    return jnp.dot(a, b, preferred_element_type=F32, precision=precision)


def _dot_nt(a, b, precision=None):
    return lax.dot_general(a, b, (((1,), (1,)), ((), ())), preferred_element_type=F32, precision=precision)


def _dot_tn(a, b, precision=None):
    return lax.dot_general(a, b, (((0,), (0,)), ((), ())), preferred_element_type=F32, precision=precision)


def _silu(x):
    return x * jax.nn.sigmoid(x)


def _softplus(x):
    return jnp.maximum(x, 0.0) + jnp.log1p(jnp.exp(-jnp.abs(x)))


def _layer_norm(y, g, b):
    mu = jnp.mean(y, axis=-1, keepdims=True)
    yc = y - mu
    var = jnp.mean(yc * yc, axis=-1, keepdims=True)
    return yc * lax.rsqrt(var + LN_EPS) * g + b


def _pick(tile, S, L):
    assert S % tile == 0, (S, tile)
    return tile


def _is_ctx_rows(i, tm, S, L):
    row = (i * tm) % S + lax.broadcasted_iota(jnp.int32, (tm, 1), 0)
    return row >= L


def _mods_kernel(c_ref, w_ref, b_ref, o_ref):
    o_ref[0] = _dot(_silu(c_ref[...]), w_ref[0], HIGHEST) + b_ref[0]


def compute_mods(c_all, w_ada, b_ada):
    depth, D, n6 = w_ada.shape
    mb = c_all.shape[0]
    tn = 1024
    return pl.pallas_call(
        _mods_kernel,
        grid=(depth, n6 // tn),
        in_specs=[pl.BlockSpec((mb, D), lambda l, j: (0, 0)),
                  pl.BlockSpec((1, D, tn), lambda l, j: (l, 0, j)),
                  pl.BlockSpec((1, 1, tn), lambda l, j: (l, 0, j))],
        out_specs=pl.BlockSpec((1, mb, tn), lambda l, j: (l, 0, j)),
        out_shape=jax.ShapeDtypeStruct((depth, mb, n6), F32),
        compiler_params=_cparams(("arbitrary", "arbitrary")),
        name="mods",
    )(c_all, w_ada, b_ada.reshape(depth, 1, n6))


def _mod_specs(parts, tm, S, B, D):
    specs = []
    for p in parts:
        specs.append(pl.BlockSpec((1, 1, 1, D), lambda i, p=p: ((i * tm) // S, p, 0, 0)))
        specs.append(pl.BlockSpec((1, 1, 1, D), lambda i, p=p: (B, p, 0, 0)))
    return specs


def _modulate_kernel(x_ref, shl_ref, shc_ref, scl_ref, scc_ref, a_ref, *, S, L):
    tm = x_ref.shape[0]
    ctx = _is_ctx_rows(pl.program_id(0), tm, S, L)
    sh = jnp.where(ctx, shc_ref[0, 0], shl_ref[0, 0])
    sc = jnp.where(ctx, scc_ref[0, 0], scl_ref[0, 0])
    a_ref[...] = (x_ref[...] * (1.0 + sc) + sh).astype(a_ref.dtype)


def modulate(x, mod4, B, S, L, tm):
    T, D = x.shape
    return pl.pallas_call(
        functools.partial(_modulate_kernel, S=S, L=L),
        grid=(T // tm,),
        in_specs=[pl.BlockSpec((tm, D), lambda i: (i, 0))] + _mod_specs((0, 1), tm, S, B, D),
        out_specs=pl.BlockSpec((tm, D), lambda i: (i, 0)),
        out_shape=jax.ShapeDtypeStruct((T, D), BF16),
        compiler_params=_cparams(("parallel",)),
        name="modulate",
    )(x, mod4, mod4, mod4, mod4)


def _mm_kernel(a_ref, w_ref, *rest, quarter):
    o_ref = rest[-1]
    acc = _dot(a_ref[...], w_ref[...])
    if quarter:
        cos_ref, sin_ref = rest[0], rest[1]
        tn = acc.shape[1]
        lane = lax.broadcasted_iota(jnp.int32, acc.shape, 1)
        first = (lane % (2 * quarter)) < quarter
        rot = jnp.where(first, pltpu.roll(acc, tn - quarter, 1), pltpu.roll(acc, quarter, 1))
        acc = acc * cos_ref[...] + rot * sin_ref[...]
    o_ref[...] = acc.astype(o_ref.dtype)


def token_matmul(a, w, tm, tn, out_dtype, rope=None, S=None):
    T, K = a.shape
    N = w.shape[1]
    assert T % tm == 0 and N % tn == 0
    in_specs = [pl.BlockSpec((tm, K), lambda i, j: (i, 0)),
                pl.BlockSpec((K, tn), lambda i, j: (0, j))]
    args = [a, w]
    quarter = 0
    if rope is not None:
        cos, sin, quarter = rope
        nper = S // tm
        in_specs += [pl.BlockSpec((tm, tn), lambda i, j: (i % nper, 0)),
                     pl.BlockSpec((tm, tn), lambda i, j: (i % nper, 0))]
        args += [cos, sin]
    return pl.pallas_call(
        functools.partial(_mm_kernel, quarter=quarter),
        grid=(T // tm, N // tn),
        in_specs=in_specs,
        out_specs=pl.BlockSpec((tm, tn), lambda i, j: (i, j)),
        out_shape=jax.ShapeDtypeStruct((T, N), out_dtype),
        compiler_params=_cparams(("parallel", "arbitrary")),
        name="token_matmul",
    )(*args)


def rope_tables(L, C, d, tn):
    half, qt = d // 2, d // 4
    rows = L // GRID_W
    row = jnp.repeat(jnp.arange(rows), GRID_W)
    col = jnp.tile(jnp.arange(GRID_W), rows)
    inv = jnp.power(ROPE_BASE, -jnp.arange(0, half, 2, dtype=F32) / half)
    ang_r = row.astype(F32)[:, None] * inv[None, :]
    ang_c = col.astype(F32)[:, None] * inv[None, :]
    ang = jnp.concatenate([ang_r, ang_r, ang_c, ang_c], axis=-1)
    sign = jnp.where((jnp.arange(d) % half) < qt, -1.0, 1.0).astype(F32)
    cos = jnp.concatenate([jnp.cos(ang), jnp.ones((C, d), F32)], axis=0)
    sin = jnp.concatenate([jnp.sin(ang) * sign[None, :], jnp.zeros((C, d), F32)], axis=0)
    return jnp.tile(cos, (1, tn // d)), jnp.tile(sin, (1, tn // d)), qt


def _diff_attn_kernel(lam_ref, q_ref, k_ref, v_ref, nw_ref, o_ref, *, L, nlat):
    qi = pl.program_id(2)
    q = q_ref[0] * (DA ** -0.5)
    lane = lax.broadcasted_iota(jnp.int32, q.shape, 1)
    q1 = jnp.where(lane < DA, q, jnp.zeros_like(q))
    q2 = jnp.where(lane >= DA, q, jnp.zeros_like(q))

    k = k_ref[0]
    v = v_ref[0]
    tq, S = q.shape[0], k.shape[0]

    vx = jnp.concatenate([v, jnp.ones_like(v)], axis=1)

    def attend(valid):
        def one_map(qm):
            s = _dot_nt(qm, k)
            if valid is not None:
                s = jnp.where(valid, s, NEG)
            e = jnp.exp((s - jnp.max(s, axis=-1, keepdims=True)).astype(BF16))
            ol = _dot(e, vx)
            return ol[:, :LANE] / ol[:, LANE:]

        o = one_map(q1) - lam_ref[0] * one_map(q2)
        o = o * lax.rsqrt(jnp.mean(o * o, axis=-1, keepdims=True) + RMS_EPS) * nw_ref[...]
        o_ref[0] = o.astype(o_ref.dtype)

    @pl.when(qi < nlat)
    def _():
        attend(None)

    @pl.when(qi >= nlat)
    def _():
        row = qi * tq + lax.broadcasted_iota(jnp.int32, (tq, 1), 0)
        kidx = lax.broadcasted_iota(jnp.int32, (tq, S), 1)
        attend((kidx >= L) | (row < L))


def diff_attention(qk_a, rest, lam, norm_w_scaled, B, S, L, HA, tq):
    nlat = L // tq
    return pl.pallas_call(
        functools.partial(_diff_attn_kernel, L=L, nlat=nlat),
        grid=(B, HA, S // tq),
        in_specs=[pl.BlockSpec(memory_space=pltpu.SMEM),
                  pl.BlockSpec((1, tq, LANE), lambda b, h, i: (b, i, h)),
                  pl.BlockSpec((1, S, LANE), lambda b, h, i: (b, 0, HA + h)),
                  pl.BlockSpec((1, S, LANE), lambda b, h, i: (b, 0, h)),
                  pl.BlockSpec((1, LANE), lambda b, h, i: (0, 0))],
        out_specs=pl.BlockSpec((1, tq, LANE), lambda b, h, i: (b, i, h)),
        out_shape=jax.ShapeDtypeStruct((B, S, HA * LANE), BF16),
        compiler_params=_cparams(("parallel", "parallel", "arbitrary")),
        name="diff_attention",
    )(lam, qk_a, qk_a, rest, norm_w_scaled)


def _swa_kernel(sink_ref, q_ref, kp_ref, kc_ref, kn_ref, kx_ref, vp_ref, vc_ref, vn_ref, vx_ref, o_ref,
                *, L, nb, G):
    h = pl.program_id(1)
    n = pl.program_id(2)
    q = jnp.concatenate([q_ref[0, :, g * DB:(g + 1) * DB] for g in range(G)], axis=0)
    k_loc = jnp.concatenate([kp_ref[0], kc_ref[0], kn_ref[0]], axis=0)
    v_loc = jnp.concatenate([vp_ref[0], vc_ref[0], vn_ref[0]], axis=0)
    rows = G * BLOCK
    scale = DB ** -0.5
    s_loc = _dot_nt(q, k_loc) * scale
    s_ctx = _dot_nt(q, kx_ref[0]) * scale
    r = lax.broadcasted_iota(jnp.int32, (rows, 3 * BLOCK), 0)
    j = lax.broadcasted_iota(jnp.int32, (rows, 3 * BLOCK), 1)
    qpos = n * BLOCK + r % BLOCK
    kpos = (n - 1) * BLOCK + j
    rel = kpos - qpos
    ok = (rel <= WINDOW) & (rel >= -WINDOW) & (kpos >= 0) & (kpos < L) & (n < nb)
    s_loc = jnp.where(ok, s_loc, NEG)
    g_of_row = lax.broadcasted_iota(jnp.int32, (rows, 1), 0) // BLOCK
    sink = jnp.zeros((rows, 1), F32)
    for g in range(G):
        sink = jnp.where(g_of_row == g, sink_ref[h * G + g], sink)
    m = jnp.maximum(jnp.maximum(jnp.max(s_loc, axis=-1, keepdims=True),
                                jnp.max(s_ctx, axis=-1, keepdims=True)), sink)
    e_loc = jnp.exp(s_loc - m)
    e_ctx = jnp.exp(s_ctx - m)
    den = (jnp.sum(e_loc, axis=-1, keepdims=True) + jnp.sum(e_ctx, axis=-1, keepdims=True)
           + jnp.exp(sink - m))
    inv = 1.0 / den
    o = _dot((e_loc * inv).astype(v_loc.dtype), v_loc) + _dot((e_ctx * inv).astype(v_loc.dtype), vx_ref[0])
    o_ref[0] = jnp.concatenate([o[g * BLOCK:(g + 1) * BLOCK] for g in range(G)], axis=1).astype(o_ref.dtype)


def swa_attention(qk_b, rest, sink, B, S, L, C, HB, v_blk0):
    G = HB // HB_KV
    nb = L // BLOCK
    xblk = L // C
    assert L % C == 0

    def loc(off, cblk):
        return lambda b, h, n: (b, jnp.clip(jnp.where(n < nb, n + off, 0), 0, nb - 1), cblk + h)

    kspec = lambda off: pl.BlockSpec((1, BLOCK, DB), loc(off, HB))
    vspec = lambda off: pl.BlockSpec((1, BLOCK, DB), loc(off, v_blk0))
    return pl.pallas_call(
        functools.partial(_swa_kernel, L=L, nb=nb, G=G),
        grid=(B, HB_KV, S // BLOCK),
        in_specs=[pl.BlockSpec(memory_space=pltpu.SMEM),
                  pl.BlockSpec((1, BLOCK, G * DB), lambda b, h, n: (b, n, h)),
                  kspec(-1), kspec(0), kspec(1),
                  pl.BlockSpec((1, C, DB), lambda b, h, n: (b, xblk, HB + h)),
                  vspec(-1), vspec(0), vspec(1),
                  pl.BlockSpec((1, C, DB), lambda b, h, n: (b, xblk, v_blk0 + h))],
        out_specs=pl.BlockSpec((1, BLOCK, G * DB), lambda b, h, n: (b, n, h)),
        out_shape=jax.ShapeDtypeStruct((B, S, HB * DB), BF16),
        compiler_params=_cparams(("parallel", "parallel", "arbitrary")),
        name="swa_attention",
    )(sink, qk_b, qk_b, qk_b, qk_b, qk_b, rest, rest, rest, rest)


def _delta_prep_kernel(x_ref, w_ref, o_ref, xs_ref, *, L, rc, mode):
    S, tc = x_ref.shape[1], x_ref.shape[2]
    pad = CONV_K // 2
    halo = 8
    xs_ref[0:halo, :] = jnp.zeros((halo, tc), F32)
    xs_ref[halo + S:halo + S + halo, :] = jnp.zeros((halo, tc), F32)
    xs_ref[halo:halo + S, :] = x_ref[0].astype(F32)
    w = w_ref[...]
    for r0 in range(0, S, rc):
        t = r0 + lax.broadcasted_iota(jnp.int32, (rc, 1), 0)
        acc = jnp.zeros((rc, tc), F32)
        for j in range(CONV_K):
            sl = xs_ref[halo + r0 + j - pad:halo + r0 + j - pad + rc, :]
            src = t + (j - pad)
            if r0 + rc == L and j > pad:
                sl = jnp.where(src < L, sl, 0.0)
            if r0 == L and j < pad:
                sl = jnp.where(src >= L, sl, 0.0)
            acc = acc + sl * w[j:j + 1, :]
        y = _silu(acc)
        if mode != "v":
            parts = []
            for h in range(tc // DKC):
                yh = y[:, h * DKC:(h + 1) * DKC]
                yh = yh * lax.rsqrt(jnp.sum(yh * yh, axis=-1, keepdims=True) + RMS_EPS)
                parts.append(yh * (DKC ** -0.5) if mode == "q" else yh)
            y = jnp.concatenate(parts, axis=1) if len(parts) > 1 else parts[0]
        o_ref[0, r0:r0 + rc, :] = y


def delta_prep(rest, conv_w_l, B, S, L, MIX_W, col_blk0, tc, rc):
    outs = []
    nt = MIX_W // tc
    for pi, mode in enumerate(("q", "k", "v")):
        outs.append(pl.pallas_call(
            functools.partial(_delta_prep_kernel, L=L, rc=rc, mode=mode),
            grid=(B, nt),
            in_specs=[pl.BlockSpec((1, S, tc), lambda b, j, pi=pi: (b, 0, col_blk0 + pi * nt + j)),
                      pl.BlockSpec((CONV_K, tc), lambda b, j, pi=pi: (0, pi * nt + j))],
            out_specs=pl.BlockSpec((1, S, tc), lambda b, j: (b, 0, j)),
            out_shape=jax.ShapeDtypeStruct((B, S, MIX_W), F32),
            scratch_shapes=[pltpu.VMEM((S + 16, tc), F32)],
            compiler_params=_cparams(("parallel", "parallel")),
            name="delta_prep_" + mode,
        )(rest, conv_w_l))
    return outs


def _split16(x):
    hi = x.astype(BF16)
    return hi, (x - hi.astype(F32)).astype(BF16)


def _prod2(x, y):
    return _dot(jnp.concatenate(x, axis=1), jnp.concatenate([y[0], y[1], y[0], y[1]], axis=0))


def _unit_tri_inverse(a_list, ii, jj):
    same16 = (ii // 16) == (jj // 16)
    same32 = (ii // 32) == (jj // 32)
    eye = (ii == jj).astype(F32)
    each = lambda f, *ls: [f(*xs) for xs in zip(*ls)]
    d = each(lambda a: _split16(jnp.where(same16, a, 0.0)), a_list)
    l1 = each(lambda a: _split16(jnp.where(same32 & jnp.logical_not(same16), a, 0.0)), a_list)
    l2 = each(lambda a: _split16(jnp.where(same32, 0.0, a)), a_list)
    d2 = each(lambda x: _split16(_prod2(x, x)), d)
    d4 = each(lambda x: _split16(_prod2(x, x)), d2)
    d8 = each(lambda x: _split16(_prod2(x, x)), d4)
    t = each(lambda a: eye - jnp.where(same16, a, 0.0), a_list)
    for p in (d2, d4, d8):
        t = each(lambda tt, pp: tt + _prod2(_split16(tt), pp), t, p)
    for l in (l1, l2):
        ts = each(_split16, t)
        m = each(lambda tt, ll: _split16(_prod2(tt, ll)), ts, l)
        t = each(lambda tf, mm, tt: tf - _prod2(mm, tt), t, m, ts)
    return t


def _delta_scan_kernel(qf_ref, kf_ref, vf_ref, qb_ref, kb_ref, vb_ref, bacf_ref, bacb_ref, barf_ref, barb_ref,
                       pc_ref, pr_ref, of_ref, ob_ref, s_ref, *, HC):
    @pl.when(pl.program_id(1) == 0)
    def _():
        s_ref[...] = jnp.zeros_like(s_ref)

    ii = lax.broadcasted_iota(jnp.int32, (CHUNK, 2 * CHUNK), 0)
    jj = lax.broadcasted_iota(jnp.int32, (CHUNK, 2 * CHUNK), 1) % CHUNK
    dirs = ((qf_ref, kf_ref, vf_ref, bacf_ref, barf_ref, of_ref), (qb_ref, kb_ref, vb_ref, bacb_ref, barb_ref, ob_ref))
    chains = []
    for dn, (q_ref, k_ref, v_ref, bac_ref, bar_ref, o_ref) in enumerate(dirs):
        ahead = (ii - jj) if dn == 0 else (jj - ii)
        strict = ahead > 0
        incl = ahead >= 0
        tri2 = incl.astype(F32)
        tri = tri2[:, :CHUNK]
        bac = bac_ref[0, 0]
        pc = pc_ref[dn]
        beta_c = jax.nn.sigmoid(bac)
        g_c = -jnp.exp(pc[0:1]) * _softplus(bac + pc[1:2])
        gc_c = _dot(tri, g_c, HIGHEST)
        gtot_c = jnp.sum(g_c, axis=0, keepdims=True)
        bar = bar_ref[0, 0, 0]
        pr = pr_ref[dn]
        g_r = -jnp.exp(pr[:, 0:1]) * _softplus(bar + pr[:, 1:2])
        gc_r = _dot_nt(g_r, jnp.concatenate([tri, tri], axis=0), HIGHEST)
        for h in range(HC):
            chains.append(dict(dn=dn, h=h, q_ref=q_ref, k_ref=k_ref, v_ref=v_ref, o_ref=o_ref, strict=strict,
                               incl=incl, beta=beta_c[:, h:h + 1], gcol=gc_c[:, HC + h:HC + h + 1],
                               grow=gc_r[HC + h:HC + h + 1, :], gtot=gtot_c[:, HC + h:HC + h + 1]))

    sl = lambda ch: slice(ch["h"] * DKC, (ch["h"] + 1) * DKC)
    for ch in chains:
        ch["k"] = ch["k_ref"][0, :, sl(ch)]
        ch["k16"] = ch["k"].astype(BF16)
        ch["kb"] = ch["k"] * ch["beta"]
        ch["dec"] = jnp.exp(jnp.where(ch["incl"], ch["gcol"] - ch["grow"], NEG))
        ch["egc"] = jnp.exp(ch["gcol"])
    for ch in chains:
        k2 = jnp.concatenate([ch["k16"], ch["k16"]], axis=0)
        ch["a"] = _dot_nt(ch["kb"].astype(BF16), k2) * jnp.where(ch["strict"], ch["dec"], 0.0)
    t_list = _unit_tri_inverse([ch["a"] for ch in chains], ii, jj)
    for ch, t in zip(chains, t_list):
        v = ch["v_ref"][0, :, sl(ch)]
        rhs = jnp.concatenate([v * ch["beta"], ch["kb"] * ch["egc"]], axis=1).astype(BF16)
        ch["uw"] = _dot(t[:, :CHUNK].astype(BF16), rhs)
    for ch in chains:
        q = ch["q_ref"][0, :, sl(ch)]
        ch["qk"] = (_dot_nt(q.astype(BF16), ch["k16"]) * ch["dec"][:, :CHUNK]).astype(BF16)
        ch["state"] = s_ref[ch["dn"], ch["h"]]
        lhs = jnp.concatenate([ch["uw"][:, DKC:], q * ch["egc"]], axis=0).astype(BF16)
        ch["ws_qs"] = _dot(lhs, ch["state"].astype(BF16))
    for ch in chains:
        ch["u16"] = (ch["uw"][:, :DKC] - ch["ws_qs"][:CHUNK]).astype(BF16)
    for ch in chains:
        ch["o_ref"][0, :, sl(ch)] = ch["ws_qs"][CHUNK:] + _dot(ch["qk"], ch["u16"])
        k_tail = (ch["k"] * jnp.exp(ch["gtot"] - ch["gcol"])).astype(BF16)
        s_ref[ch["dn"], ch["h"]] = ch["state"] * jnp.exp(ch["gtot"]) + _dot_tn(k_tail, ch["u16"])


def delta_scan(qc, kc, vc, bac, bar, pc, pr, B, S, L, HC):
    nch = S // CHUNK
    nl = L // CHUNK
    fwd = lambda b, c: (b, (c + nl) % nch, 0)
    bwd = lambda b, c: (b, nch - 1 - c, 0)
    mw = HC * DKC
    row = lambda m: pl.BlockSpec((1, CHUNK, mw), m)
    bac_spec = lambda dn, m: pl.BlockSpec((1, 1, CHUNK, LANE), lambda b, c: (dn,) + m(b, c))
    bar_spec = lambda dn, m: pl.BlockSpec((1, 1, 1, 2 * HC, CHUNK), lambda b, c: (dn,) + m(b, c) + (0,))
    return pl.pallas_call(
        functools.partial(_delta_scan_kernel, HC=HC),
        grid=(B, nch),
        in_specs=[row(fwd), row(fwd), row(fwd), row(bwd), row(bwd), row(bwd),
                  bac_spec(0, fwd), bac_spec(1, bwd), bar_spec(0, fwd), bar_spec(1, bwd),
                  pl.BlockSpec((2, 8, LANE), lambda b, c: (0, 0, 0)),
                  pl.BlockSpec((2, 2 * HC, LANE), lambda b, c: (0, 0, 0))],
        out_specs=[row(fwd), row(bwd)],
        out_shape=[jax.ShapeDtypeStruct((B, S, mw), F32), jax.ShapeDtypeStruct((B, S, mw), F32)],
        scratch_shapes=[pltpu.VMEM((2, HC, DKC, DKC), F32)],
        compiler_params=_cparams(("parallel", "arbitrary")),
        name="delta_scan",
    )(qc, kc, vc, qc, kc, vc, bac, bac, bar, bar, pc, pr)


def _delta_post_kernel(of_ref, ob_ref, z_ref, nw_ref, y_ref):
    o = of_ref[...] + ob_ref[...]
    z = z_ref[...].astype(F32)
    nw = nw_ref[...]
    parts = []
    for h in range(o.shape[1] // DKC):
        oh = o[:, h * DKC:(h + 1) * DKC]
        oh = oh * lax.rsqrt(jnp.mean(oh * oh, axis=-1, keepdims=True) + RMS_EPS) * nw
        parts.append(oh * _silu(z[:, h * DKC:(h + 1) * DKC]))
    y_ref[...] = jnp.concatenate(parts, axis=1).astype(y_ref.dtype)


def delta_post(o_f, o_b, rest, norm_w, T, MIX_W, z_blk, tm):
    return pl.pallas_call(
        _delta_post_kernel,
        grid=(T // tm,),
        in_specs=[pl.BlockSpec((tm, MIX_W), lambda i: (i, 0)),
                  pl.BlockSpec((tm, MIX_W), lambda i: (i, 0)),
                  pl.BlockSpec((tm, MIX_W), lambda i: (i, z_blk)),
                  pl.BlockSpec((1, DKC), lambda i: (0, 0))],
        out_specs=pl.BlockSpec((tm, MIX_W), lambda i: (i, 0)),
        out_shape=jax.ShapeDtypeStruct((T, MIX_W), BF16),
        compiler_params=_cparams(("parallel",)),
        name="delta_post",
    )(o_f, o_b, rest, norm_w)


def _merge_kernel(oa_ref, ob_ref, oc_ref, w_ref, ga_ref, gb_ref, gc_ref, m_ref):
    acc = jax.nn.sigmoid(ga_ref[...].astype(F32)) * _dot(oa_ref[...], w_ref[0])
    acc = acc + jax.nn.sigmoid(gb_ref[...].astype(F32)) * _dot(ob_ref[...], w_ref[1])
    acc = acc + jax.nn.sigmoid(gc_ref[...].astype(F32)) * _dot(oc_ref[...], w_ref[2])
    m_ref[...] = acc.astype(m_ref.dtype)


def merge_branches(oa, ob, oc, w_branch, rest, T, D, MIX_W, gate_blk0, tm, tn):
    nd = D // tn
    o_spec = pl.BlockSpec((tm, MIX_W), lambda i, j: (i, 0))
    g_spec = lambda br: pl.BlockSpec((tm, tn), lambda i, j, br=br: (i, gate_blk0 + br * nd + j))
    return pl.pallas_call(
        _merge_kernel,
        grid=(T // tm, nd),
        in_specs=[o_spec, o_spec, o_spec,
                  pl.BlockSpec((3, MIX_W, tn), lambda i, j: (0, 0, j)),
                  g_spec(0), g_spec(1), g_spec(2)],
        out_specs=pl.BlockSpec((tm, tn), lambda i, j: (i, j)),
        out_shape=jax.ShapeDtypeStruct((T, D), BF16),
        compiler_params=_cparams(("parallel", "arbitrary")),
        name="merge_branches",
    )(oa, ob, oc, w_branch, rest, rest, rest)


def _pack_rows(y):
    half = y.shape[1] // 2

    def rne_bits(v):
        b = lax.bitcast_convert_type(v, jnp.uint32)
        return b + jnp.uint32(0x7FFF) + ((b >> 16) & jnp.uint32(1))

    return (rne_bits(y[:, half:]) & jnp.uint32(0xFFFF0000)) | (rne_bits(y[:, :half]) >> 16)


def _unpack_rows(w):
    lo = lax.bitcast_convert_type(w << 16, F32)
    hi = lax.bitcast_convert_type(w & jnp.uint32(0xFFFF0000), F32)
    return lo, hi


def _outproj_ln_kernel(m_ref, w_ref, x_ref, g_ref, b_ref, gl_ref, gc_ref, shl_ref, shc_ref, scl_ref, scc_ref,
                       x1_ref, a2_ref, a2p_ref, *, S, L, alpha):
    tm = x_ref.shape[0]
    ctx = _is_ctx_rows(pl.program_id(0), tm, S, L)
    gate = jnp.where(ctx, gc_ref[0, 0], gl_ref[0, 0])
    y = alpha * x_ref[...] + gate * _dot(m_ref[...], w_ref[...])
    x1 = _layer_norm(y, g_ref[...], b_ref[...])
    x1_ref[...] = x1
    sh = jnp.where(ctx, shc_ref[0, 0], shl_ref[0, 0])
    sc = jnp.where(ctx, scc_ref[0, 0], scl_ref[0, 0])
    a2 = x1 * (1.0 + sc) + sh
    a2_ref[...] = a2
    a2p_ref[...] = _pack_rows(a2)


def outproj_ln(merged, w_out, x, ln_g, ln_b, mod4, B, S, L, alpha, tm):
    T, D = x.shape
    row = pl.BlockSpec((tm, D), lambda i: (i, 0))
    vec = pl.BlockSpec((1, D), lambda i: (0, 0))
    return pl.pallas_call(
        functools.partial(_outproj_ln_kernel, S=S, L=L, alpha=alpha),
        grid=(T // tm,),
        in_specs=[row, pl.BlockSpec((D, D), lambda i: (0, 0)), row, vec, vec] + _mod_specs((2, 3, 4), tm, S, B, D),
        out_specs=[row, row, pl.BlockSpec((tm, D // 2), lambda i: (i, 0))],
        out_shape=[jax.ShapeDtypeStruct((T, D), F32), jax.ShapeDtypeStruct((T, D), F32),
                   jax.ShapeDtypeStruct((T, D // 2), jnp.uint32)],
        compiler_params=_cparams(("parallel",)),
        name="outproj_ln",
    )(merged, w_out, x, ln_g, ln_b, *([mod4] * 6))


def _router_kernel(a_ref, w_ref, bias_ref, gate_ref, idx_ref, *, E):
    a_hi, a_lo = _split16(a_ref[...])
    w_hi, w_lo = _split16(w_ref[...])
    logits = _dot(a_hi, w_hi) + (_dot(a_hi, w_lo) + _dot(a_lo, w_hi))
    scores = jax.nn.sigmoid(logits)
    lane = lax.broadcasted_iota(jnp.int32, scores.shape, 1)
    work = jnp.where(lane < E, scores + bias_ref[...], NEG)
    chosen = jnp.zeros(scores.shape, jnp.bool_)
    idx = jnp.zeros(scores.shape, jnp.int32)
    for kk in range(TOP_K):
        m = jnp.max(work, axis=-1, keepdims=True)
        first = jnp.min(jnp.where(work == m, lane, LANE), axis=-1, keepdims=True)
        pick = lane == first
        chosen = jnp.logical_or(chosen, pick)
        idx = jnp.where(lane == kk, first, idx)
        work = jnp.where(pick, 2.0 * NEG, work)
    sel = jnp.where(chosen, scores, 0.0)
    gate_ref[...] = sel / jnp.sum(sel, axis=-1, keepdims=True) * ROUTED_SCALE
    idx_ref[...] = idx


def router(a2, w_router_pad, bias_pad, E, tm):
    T, D = a2.shape
    return pl.pallas_call(
        functools.partial(_router_kernel, E=E),
        grid=(T // tm,),
        in_specs=[pl.BlockSpec((tm, D), lambda i: (i, 0)),
                  pl.BlockSpec((D, LANE), lambda i: (0, 0)),
                  pl.BlockSpec((1, LANE), lambda i: (0, 0))],
        out_specs=[pl.BlockSpec((tm, LANE), lambda i: (i, 0)), pl.BlockSpec((tm, LANE), lambda i: (i, 0))],
        out_shape=[jax.ShapeDtypeStruct((T, LANE), F32), jax.ShapeDtypeStruct((T, LANE), jnp.int32)],
        compiler_params=_cparams(("parallel",)),
        name="router",
    )(a2, w_router_pad, bias_pad)


def route_plan(idx8, gate, E, tme):
    T = idx8.shape[0]
    P = T * TOP_K
    assert T >= 3 * tme
    nt = -(-P // tme) + E + 1
    e_flat = idx8.reshape(P)
    sorted_p = jnp.sort(e_flat * P + jnp.arange(P, dtype=jnp.int32)) % P
    experts = jnp.arange(E, dtype=jnp.int32)
    counts = jnp.sum((e_flat[:, None] == experts[None, :]).astype(jnp.int32), axis=0)
    starts = jnp.cumsum(counts) - counts
    tiles_e = (counts + tme - 1) // tme
    tile_cum = jnp.cumsum(tiles_e)
    n_used = tile_cum[-1:].astype(jnp.int32)
    tile_ids = jnp.arange(nt, dtype=jnp.int32)
    te = jnp.minimum(jnp.sum((tile_ids[:, None] >= tile_cum[None, :]).astype(jnp.int32), axis=1), E - 1)
    local = tile_ids - (tile_cum - tiles_e)[te]
    lane = jnp.arange(tme, dtype=jnp.int32)[None, :]
    row_in_e = local[:, None] * tme + lane
    valid = (row_in_e < counts[te][:, None]) & (tile_ids[:, None] < n_used)
    pair = sorted_p[jnp.clip(starts[te][:, None] + row_in_e, 0, P - 1)]
    tok = jnp.where(valid, pair // TOP_K, 0)
    dummy = P + jnp.where(tile_ids < n_used, tile_ids % 2, 2)[:, None] * tme + lane
    dst = jnp.where(valid, (pair % TOP_K) * T + pair // TOP_K, dummy)
    wts = jnp.take_along_axis(gate, idx8, axis=1).reshape(P)
    w = jnp.where(valid, wts[pair], 0.0).reshape(nt * tme, 1)
    return te, n_used, tok.reshape(nt, 1, tme), dst.reshape(nt, 1, tme), w


def _experts_kernel(te_ref, nu_ref, tok0_ref, tokn_ref, dstp_ref, w_ref, wg_ref, wu_ref, wd_ref, x_hbm, y_hbm,
                    xb0, xb1, yb0, yb1, gsem, ssem, *, tme):
    i = pl.program_id(0)
    nu = nu_ref[0]
    xbufs, ybufs = (xb0, xb1), (yb0, yb1)

    def row_in(t, s, r):
        return pltpu.make_async_copy(x_hbm.at[pl.ds(t, 1)], xbufs[s].at[pl.ds(r, 1)], gsem.at[s])

    def row_out(t, s, r):
        return pltpu.make_async_copy(ybufs[s].at[pl.ds(r, 1)], y_hbm.at[pl.ds(t, 1)], ssem.at[s])

    def wait_in(s):
        pltpu.make_async_copy(x_hbm.at[pl.ds(0, tme)], xbufs[s], gsem.at[s]).wait()

    def wait_out(s):
        pltpu.make_async_copy(ybufs[s], y_hbm.at[pl.ds(0, tme)], ssem.at[s]).wait()

    @pl.when(i == 0)
    def _():
        yb1[...] = jnp.zeros_like(yb1)

        def body(r, carry):
            row_in(tok0_ref[0, 0, r], 0, r).start()
            return carry
        lax.fori_loop(0, tme, body, 0, unroll=8)

    def step(s, compute):
        wait_in(s)

        @pl.when(i >= 1)
        def _():
            wait_out(s)

        if compute:
            lo, hi = _unpack_rows(xbufs[s][...])
            x16 = jnp.concatenate([lo, hi], axis=1).astype(BF16)
            hid = (_silu(_dot(x16, wg_ref[0])) * _dot(x16, wu_ref[0]) * w_ref[...]).astype(BF16)
            ybufs[s][...] = _pack_rows(_dot(hid, wd_ref[0]))
        for r in range(tme):
            row_out(dstp_ref[0, 0, r], 1 - s, r).start(priority=r % 2)
            if compute:
                row_in(tokn_ref[0, 0, r], 1 - s, r).start(priority=r % 2)
        if not compute:
            wait_out(1 - s)

    for s in range(2):
        @pl.when((i < nu) & (i % 2 == s))
        def _(s=s):
            step(s, True)

        @pl.when((i == nu) & (i % 2 == s))
        def _(s=s):
            step(s, False)


def experts_routed(a2p, plan, wg, wu, wd, tme):
    te, n_used, tok, dst, w = plan
    T, dh = a2p.shape
    E, D, de = wg.shape
    nt = tok.shape[0]
    P = T * TOP_K
    smem_tile = lambda m: pl.BlockSpec((1, 1, tme), m, memory_space=pltpu.SMEM)
    grid_spec = pltpu.PrefetchScalarGridSpec(
        num_scalar_prefetch=2,
        grid=(nt,),
        in_specs=[smem_tile(lambda i, te, nu: (0, 0, 0)),
                  smem_tile(lambda i, te, nu: (jnp.minimum(i + 1, nt - 1), 0, 0)),
                  smem_tile(lambda i, te, nu: (jnp.where(i == 0, nt - 1, i - 1), 0, 0)),
                  pl.BlockSpec((tme, 1), lambda i, te, nu: (i, 0)),
                  pl.BlockSpec((1, D, de), lambda i, te, nu: (te[i], 0, 0)),
                  pl.BlockSpec((1, D, de), lambda i, te, nu: (te[i], 0, 0)),
                  pl.BlockSpec((1, de, D), lambda i, te, nu: (te[i], 0, 0)),
                  pl.BlockSpec(memory_space=pl.ANY)],
        out_specs=pl.BlockSpec(memory_space=pl.ANY),
        scratch_shapes=[pltpu.VMEM((tme, dh), jnp.uint32), pltpu.VMEM((tme, dh), jnp.uint32),
                        pltpu.VMEM((tme, dh), jnp.uint32), pltpu.VMEM((tme, dh), jnp.uint32),
                        pltpu.SemaphoreType.DMA((2,)), pltpu.SemaphoreType.DMA((2,))])
    return pl.pallas_call(
        functools.partial(_experts_kernel, tme=tme),
        grid_spec=grid_spec,
        out_shape=jax.ShapeDtypeStruct((P + T, dh), jnp.uint32),
        compiler_params=_cparams(("arbitrary",)),
        name="experts_routed",
    )(te, n_used, tok, tok, dst, w, wg, wu, wd, a2p)


def _shared_ln_kernel(a_ref, f_ref, x_ref, wgu_ref, wd_ref, g_ref, b_ref, gl_ref, gc_ref, shl_ref, shc_ref,
                      scl_ref, scc_ref, x2_ref, a1_ref, *, S, L, alpha, DS):
    tm = x_ref.shape[0]
    ctx = _is_ctx_rows(pl.program_id(0), tm, S, L)
    gu = _dot(a_ref[...].astype(BF16), wgu_ref[...])
    f = _dot((_silu(gu[:, :DS]) * gu[:, DS:]).astype(BF16), wd_ref[...])
    lo, hi = _unpack_rows(f_ref[0])
    for k in range(1, TOP_K):
        lo_k, hi_k = _unpack_rows(f_ref[k])
        lo, hi = lo + lo_k, hi + hi_k
    f = f + jnp.concatenate([lo, hi], axis=1)
    gate = jnp.where(ctx, gc_ref[0, 0], gl_ref[0, 0])
    x2 = _layer_norm(alpha * x_ref[...] + gate * f, g_ref[...], b_ref[...])
    x2_ref[...] = x2
    sh = jnp.where(ctx, shc_ref[0, 0], shl_ref[0, 0])
    sc = jnp.where(ctx, scc_ref[0, 0], scl_ref[0, 0])
    a1_ref[...] = (x2 * (1.0 + sc) + sh).astype(a1_ref.dtype)


def shared_ln(a2, f_routed, x1, wgu_s, wd_s, ln_g, ln_b, mod4, mod4_next, B, S, L, alpha, tm):
    T, D = x1.shape
    ds2 = wgu_s.shape[1]
    row = pl.BlockSpec((tm, D), lambda i: (i, 0))
    vec = pl.BlockSpec((1, D), lambda i: (0, 0))
    return pl.pallas_call(
        functools.partial(_shared_ln_kernel, S=S, L=L, alpha=alpha, DS=ds2 // 2),
        grid=(T // tm,),
        in_specs=[row, pl.BlockSpec((TOP_K, tm, D // 2), lambda i: (0, i, 0)), row,
                  pl.BlockSpec((D, ds2), lambda i: (0, 0)),
                  pl.BlockSpec((ds2 // 2, D), lambda i: (0, 0)),
                  vec, vec] + _mod_specs((5,), tm, S, B, D) + _mod_specs((0, 1), tm, S, B, D),
        out_specs=[row, row],
        out_shape=[jax.ShapeDtypeStruct((T, D), F32), jax.ShapeDtypeStruct((T, D), BF16)],
        compiler_params=_cparams(("parallel",)),
        name="shared_ln",
    )(a2, f_routed, x1, wgu_s, wd_s, ln_g, ln_b, mod4, mod4, *([mod4_next] * 4))


def _col_layout(MIX_W, D, HC):
    kv = (MIX_W, MIX_W, HB_KV * DB, HB_KV * DB, MIX_W, MIX_W, MIX_W, 2 * HC, 2 * HC)
    qs = (MIX_W, MIX_W, MIX_W, 3 * D)
    names = ("ak", "av", "bk", "bv", "cq", "ck", "cv", "cb", "ca", "qa", "qb", "z", "gates")
    offs, o = {}, 0
    for n, s in zip(names, kv + qs):
        offs[n] = (o, o + s)
        o += s
    return offs


def kernel(x, c, ctx, c_ctx, w_ada, b_ada, w_in, lam_q1, lam_k1, lam_q2, lam_k2, diff_norm_w, sink, conv_w, a_log, dt_bias, delta_norm_w, w_branch, w_out, ln1_g, ln1_b, ln2_g, ln2_b, w_router, router_bias, w_gate_e, w_up_e, w_down_e, w_gate_s, w_up_s, w_down_s):
    B, L, D = x.shape
    C = ctx.shape[1]
    S = L + C
    T = B * S
    depth = w_ada.shape[0]
    MIX_W = D // 2
    HA = MIX_W // (2 * DA)
    HB = MIX_W // DB
    HC = MIX_W // DKC
    E = w_router.shape[-1]
    alpha = (2 * depth) ** 0.25
    assert 2 * HC <= LANE and E <= LANE

    tm_tok = math.gcd(256, math.gcd(L, C))
    tm_mm = S // 3 if S % 3 == 0 and (S // 3) % 16 == 0 else tm_tok
    tn_mm = min(512, MIX_W)
    tq = tm_mm
    if S % 2 == 0 and (S // 2) % 16 == 0:
        tm_mm = S // 2
    tc = min(512, MIX_W)
    rc = math.gcd(256, math.gcd(L, C))
    tme = 256

    mb = -(-(B + 1) // 8) * 8
    c_all = jnp.zeros((mb, D), F32).at[:B].set(c).at[B].set(c_ctx)
    mods = compute_mods(c_all, w_ada, b_ada)
    mod4 = [mods[l].reshape(mb, 6, 1, D) for l in range(depth)]

    cos_a, sin_a, qt_a = rope_tables(L, C, DA, tn_mm)
    cos_b, sin_b, qt_b = rope_tables(L, C, DB, tn_mm if (HB + HB_KV) * DB % tn_mm == 0 else DB * HB_KV)
    tn_b = cos_b.shape[1]

    offs = _col_layout(MIX_W, D, HC)
    col = lambda w, n: w[:, offs[n][0]:offs[n][1]]
    n_rest = 5 * MIX_W + 3 * D + HB_KV * DB
    n_rest_pad = -(-n_rest // tn_mm) * tn_mm
    z_blk = 4
    gate_off = 5 * MIX_W
    bv_off = 5 * MIX_W + 3 * D

    h = jnp.concatenate([x, ctx], axis=1).reshape(T, D)
    a1 = modulate(h, mod4[0], B, S, L, tm_tok)

    for l in range(depth):
        wl = w_in[l].astype(BF16)
        w_qk_a = jnp.concatenate([col(wl, "qa"), col(wl, "ak")], axis=1)
        w_qk_b = jnp.concatenate([col(wl, "qb"), col(wl, "bk")], axis=1)
        w_rest = jnp.concatenate([col(wl, n) for n in ("av", "cq", "ck", "cv", "z", "gates", "bv")]
                                 + [jnp.zeros((D, n_rest_pad - n_rest), BF16)], axis=1)
        w_ba = jnp.concatenate([col(wl, "cb"), col(wl, "ca"), jnp.zeros((D, LANE - 4 * HC), BF16)], axis=1)

        qk_a = token_matmul(a1, w_qk_a, tm_mm, tn_mm, BF16, rope=(cos_a, sin_a, qt_a), S=S)
        qk_b = token_matmul(a1, w_qk_b, tm_mm, tn_b, BF16, rope=(cos_b, sin_b, qt_b), S=S)
        rest = token_matmul(a1, w_rest, tm_mm, tn_mm, BF16)
        ba = token_matmul(a1, w_ba, tm_mm, LANE, F32)

        lam_init = 0.8 - 0.6 * math.exp(-0.3 * l)
        lam = (jnp.exp(jnp.sum(lam_q1[l] * lam_k1[l])) - jnp.exp(jnp.sum(lam_q2[l] * lam_k2[l]))
               + lam_init).reshape(1).astype(F32)
        nw_a = (diff_norm_w[l] * (1.0 - lam_init)).reshape(1, 2 * DA)
        o_a = diff_attention(qk_a.reshape(B, S, -1), rest.reshape(B, S, -1), lam, nw_a, B, S, L, HA, tq)

        o_b = swa_attention(qk_b.reshape(B, S, -1), rest.reshape(B, S, -1), sink[l], B, S, L, C, HB,
                            bv_off // DB)

        qc, kc, vc = delta_prep(rest.reshape(B, S, -1), conv_w[l], B, S, L, MIX_W, MIX_W // tc, tc, rc)
        ba3 = ba.reshape(B, S, LANE)
        cb = ba3[..., :2 * HC].reshape(B, S, 2, HC)
        ca = ba3[..., 2 * HC:4 * HC].reshape(B, S, 2, HC)
        bdir = jnp.moveaxis(jnp.concatenate([cb, ca], axis=-1), 2, 0)
        bac = jnp.pad(bdir, ((0, 0), (0, 0), (0, 0), (0, LANE - 2 * HC)))
        bar = jnp.swapaxes(bdir.reshape(2, B, S // CHUNK, CHUNK, 2 * HC), -1, -2)
        pvec = jnp.stack([a_log[l], dt_bias[l]], axis=1).astype(F32)
        pc = jnp.zeros((2, 8, LANE), F32).at[:, :2, HC:2 * HC].set(pvec)
        pr = jnp.zeros((2, 2 * HC, LANE), F32).at[:, HC:, :2].set(jnp.swapaxes(pvec, 1, 2))
        o_cf, o_cb = delta_scan(qc, kc, vc, bac, bar, pc, pr, B, S, L, HC)
        o_c = delta_post(o_cf.reshape(T, MIX_W), o_cb.reshape(T, MIX_W), rest, delta_norm_w[l].reshape(1, DKC), T, MIX_W, z_blk,
                         tm_tok)

        merged = merge_branches(o_a.reshape(T, MIX_W), o_b.reshape(T, MIX_W), o_c, w_branch[l].astype(BF16),
                                rest, T, D, MIX_W, gate_off // tn_mm, tm_mm, tn_mm)
        x1, a2, a2p = outproj_ln(merged, w_out[l].astype(BF16), h, ln1_g[l].reshape(1, D), ln1_b[l].reshape(1, D),
                            mod4[l], B, S, L, alpha, tm_tok)

        w_r = jnp.pad(w_router[l], ((0, 0), (0, LANE - E)))
        b_r = jnp.pad(router_bias[l], (0, LANE - E)).reshape(1, LANE)
        gate, idx = router(a2, w_r, b_r, E, tm_tok)
        plan = route_plan(idx[:, :TOP_K], gate, E, tme)
        f_routed = experts_routed(a2p, plan, w_gate_e[l].astype(BF16), w_up_e[l].astype(BF16),
                                  w_down_e[l].astype(BF16), tme).reshape(TOP_K + 1, T, D // 2)
        wgu_s = jnp.concatenate([w_gate_s[l], w_up_s[l]], axis=-1).astype(BF16)
        h, a1 = shared_ln(a2, f_routed, x1, wgu_s, w_down_s[l].astype(BF16), ln2_g[l].reshape(1, D),
                          ln2_b[l].reshape(1, D), mod4[l], mod4[min(l + 1, depth - 1)], B, S, L, alpha, tm_tok)

    return h.reshape(B, S, D)[:, :L]
```

```python
import functools
import math

import jax
import jax.numpy as jnp
from jax import lax
from jax.experimental import pallas as pl
from jax.experimental.pallas import tpu as pltpu

F32 = jnp.float32
BF16 = jnp.bfloat16
HIGHEST = lax.Precision.HIGHEST

LANE = 128
DA = 64
DB = 128
HB_KV = 2
DKC = 128
CONV_K = 5
CHUNK = 64
BLOCK = 128
WINDOW = 128
GRID_W = 64
ROPE_BASE = 10000.0
TOP_K = 8
ROUTED_SCALE = 2.5
LN_EPS = 1e-5
RMS_EPS = 1e-6
NEG = -1e30
VMEM_LIMIT = 56 * 1024 * 1024


def _cparams(sem):
    return pltpu.CompilerParams(dimension_semantics=sem, vmem_limit_bytes=VMEM_LIMIT)


def _dot(a, b, precision=None):
    return jnp.dot(a, b, preferred_element_type=F32, precision=precision)


def _dot_nt(a, b, precision=None):
    return lax.dot_general(a, b, (((1,), (1,)), ((), ())), preferred_element_type=F32, precision=precision)


def _dot_tn(a, b, precision=None):
    return lax.dot_general(a, b, (((0,), (0,)), ((), ())), preferred_element_type=F32, precision=precision)


def _silu(x):
    return x * jax.nn.sigmoid(x)


def _softplus(x):
    return jnp.maximum(x, 0.0) + jnp.log1p(jnp.exp(-jnp.abs(x)))


def _layer_norm(y, g, b):
    mu = jnp.mean(y, axis=-1, keepdims=True)
    yc = y - mu
    var = jnp.mean(yc * yc, axis=-1, keepdims=True)
    return yc * lax.rsqrt(var + LN_EPS) * g + b


def _pick(tile, S, L):
    assert S % tile == 0, (S, tile)
    return tile


def _is_ctx_rows(i, tm, S, L):
    row = (i * tm) % S + lax.broadcasted_iota(jnp.int32, (tm, 1), 0)
    return row >= L


def _mods_kernel(c_ref, w_ref, b_ref, o_ref):
    o_ref[0] = _dot(_silu(c_ref[...]), w_ref[0], HIGHEST) + b_ref[0]


def compute_mods(c_all, w_ada, b_ada):
    depth, D, n6 = w_ada.shape
    mb = c_all.shape[0]
    tn = 1024
    return pl.pallas_call(
        _mods_kernel,
        grid=(depth, n6 // tn),
        in_specs=[pl.BlockSpec((mb, D), lambda l, j: (0, 0)),
                  pl.BlockSpec((1, D, tn), lambda l, j: (l, 0, j)),
                  pl.BlockSpec((1, 1, tn), lambda l, j: (l, 0, j))],
        out_specs=pl.BlockSpec((1, mb, tn), lambda l, j: (l, 0, j)),
        out_shape=jax.ShapeDtypeStruct((depth, mb, n6), F32),
        compiler_params=_cparams(("arbitrary", "arbitrary")),
        name="mods",
    )(c_all, w_ada, b_ada.reshape(depth, 1, n6))


def _mod_specs(parts, tm, S, B, D):
    specs = []
    for p in parts:
        specs.append(pl.BlockSpec((1, 1, 1, D), lambda i, p=p: ((i * tm) // S, p, 0, 0)))
        specs.append(pl.BlockSpec((1, 1, 1, D), lambda i, p=p: (B, p, 0, 0)))
    return specs


def _modulate_kernel(x_ref, shl_ref, shc_ref, scl_ref, scc_ref, a_ref, *, S, L):
    tm = x_ref.shape[0]
    ctx = _is_ctx_rows(pl.program_id(0), tm, S, L)
    sh = jnp.where(ctx, shc_ref[0, 0], shl_ref[0, 0])
    sc = jnp.where(ctx, scc_ref[0, 0], scl_ref[0, 0])
    a_ref[...] = (x_ref[...] * (1.0 + sc) + sh).astype(a_ref.dtype)


def modulate(x, mod4, B, S, L, tm):
    T, D = x.shape
    return pl.pallas_call(
        functools.partial(_modulate_kernel, S=S, L=L),
        grid=(T // tm,),
        in_specs=[pl.BlockSpec((tm, D), lambda i: (i, 0))] + _mod_specs((0, 1), tm, S, B, D),
        out_specs=pl.BlockSpec((tm, D), lambda i: (i, 0)),
        out_shape=jax.ShapeDtypeStruct((T, D), BF16),
        compiler_params=_cparams(("parallel",)),
        name="modulate",
    )(x, mod4, mod4, mod4, mod4)


def _mm_kernel(a_ref, w_ref, *rest, quarter):
    o_ref = rest[-1]
    acc = _dot(a_ref[...], w_ref[...])
    if quarter:
        cos_ref, sin_ref = rest[0], rest[1]
        tn = acc.shape[1]
        lane = lax.broadcasted_iota(jnp.int32, acc.shape, 1)
        first = (lane % (2 * quarter)) < quarter
        rot = jnp.where(first, pltpu.roll(acc, tn - quarter, 1), pltpu.roll(acc, quarter, 1))
        acc = acc * cos_ref[...] + rot * sin_ref[...]
    o_ref[...] = acc.astype(o_ref.dtype)


def token_matmul(a, w, tm, tn, out_dtype, rope=None, S=None):
    T, K = a.shape
    N = w.shape[1]
    assert T % tm == 0 and N % tn == 0
    in_specs = [pl.BlockSpec((tm, K), lambda i, j: (i, 0)),
                pl.BlockSpec((K, tn), lambda i, j: (0, j))]
    args = [a, w]
    quarter = 0
    if rope is not None:
        cos, sin, quarter = rope
        nper = S // tm
        in_specs += [pl.BlockSpec((tm, tn), lambda i, j: (i % nper, 0)),
                     pl.BlockSpec((tm, tn), lambda i, j: (i % nper, 0))]
        args += [cos, sin]
    return pl.pallas_call(
        functools.partial(_mm_kernel, quarter=quarter),
        grid=(T // tm, N // tn),
        in_specs=in_specs,
        out_specs=pl.BlockSpec((tm, tn), lambda i, j: (i, j)),
        out_shape=jax.ShapeDtypeStruct((T, N), out_dtype),
        compiler_params=_cparams(("parallel", "arbitrary")),
        name="token_matmul",
    )(*args)


def rope_tables(L, C, d, tn):
    half, qt = d // 2, d // 4
    rows = L // GRID_W
    row = jnp.repeat(jnp.arange(rows), GRID_W)
    col = jnp.tile(jnp.arange(GRID_W), rows)
    inv = jnp.power(ROPE_BASE, -jnp.arange(0, half, 2, dtype=F32) / half)
    ang_r = row.astype(F32)[:, None] * inv[None, :]
    ang_c = col.astype(F32)[:, None] * inv[None, :]
    ang = jnp.concatenate([ang_r, ang_r, ang_c, ang_c], axis=-1)
    sign = jnp.where((jnp.arange(d) % half) < qt, -1.0, 1.0).astype(F32)
    cos = jnp.concatenate([jnp.cos(ang), jnp.ones((C, d), F32)], axis=0)
    sin = jnp.concatenate([jnp.sin(ang) * sign[None, :], jnp.zeros((C, d), F32)], axis=0)
    return jnp.tile(cos, (1, tn // d)), jnp.tile(sin, (1, tn // d)), qt


def _diff_attn_kernel(lam_ref, q_ref, k_ref, v_ref, nw_ref, o_ref, *, L, nlat):
    qi = pl.program_id(2)
    q = q_ref[0] * (DA ** -0.5)
    lane = lax.broadcasted_iota(jnp.int32, q.shape, 1)
    q1 = jnp.where(lane < DA, q, jnp.zeros_like(q))
    q2 = jnp.where(lane >= DA, q, jnp.zeros_like(q))

    k = k_ref[0]
    v = v_ref[0]
    tq, S = q.shape[0], k.shape[0]

    vx = jnp.concatenate([v, jnp.ones_like(v)], axis=1)

    def attend(valid):
        def one_map(qm):
            s = _dot_nt(qm, k)
            if valid is not None:
                s = jnp.where(valid, s, NEG)
            e = jnp.exp((s - jnp.max(s, axis=-1, keepdims=True)).astype(BF16))
            ol = _dot(e, vx)
            return ol[:, :LANE] / ol[:, LANE:]

        o = one_map(q1) - lam_ref[0] * one_map(q2)
        o = o * lax.rsqrt(jnp.mean(o * o, axis=-1, keepdims=True) + RMS_EPS) * nw_ref[...]
        o_ref[0] = o.astype(o_ref.dtype)

    @pl.when(qi < nlat)
    def _():
        attend(None)

    @pl.when(qi >= nlat)
    def _():
        row = qi * tq + lax.broadcasted_iota(jnp.int32, (tq, 1), 0)
        kidx = lax.broadcasted_iota(jnp.int32, (tq, S), 1)
        attend((kidx >= L) | (row < L))


def diff_attention(qk_a, rest, lam, norm_w_scaled, B, S, L, HA, tq):
    nlat = L // tq
    return pl.pallas_call(
        functools.partial(_diff_attn_kernel, L=L, nlat=nlat),
        grid=(B, HA, S // tq),
        in_specs=[pl.BlockSpec(memory_space=pltpu.SMEM),
                  pl.BlockSpec((1, tq, LANE), lambda b, h, i: (b, i, h)),
                  pl.BlockSpec((1, S, LANE), lambda b, h, i: (b, 0, HA + h)),
                  pl.BlockSpec((1, S, LANE), lambda b, h, i: (b, 0, h)),
                  pl.BlockSpec((1, LANE), lambda b, h, i: (0, 0))],
        out_specs=pl.BlockSpec((1, tq, LANE), lambda b, h, i: (b, i, h)),
        out_shape=jax.ShapeDtypeStruct((B, S, HA * LANE), BF16),
        compiler_params=_cparams(("parallel", "parallel", "arbitrary")),
        name="diff_attention",
    )(lam, qk_a, qk_a, rest, norm_w_scaled)


def _swa_kernel(sink_ref, q_ref, kp_ref, kc_ref, kn_ref, kx_ref, vp_ref, vc_ref, vn_ref, vx_ref, o_ref,
                *, L, nb, G):
    h = pl.program_id(1)
    n = pl.program_id(2)
    q = jnp.concatenate([q_ref[0, :, g * DB:(g + 1) * DB] for g in range(G)], axis=0)
    k_loc = jnp.concatenate([kp_ref[0], kc_ref[0], kn_ref[0]], axis=0)
    v_loc = jnp.concatenate([vp_ref[0], vc_ref[0], vn_ref[0]], axis=0)
    rows = G * BLOCK
    scale = DB ** -0.5
    s_loc = _dot_nt(q, k_loc) * scale
    s_ctx = _dot_nt(q, kx_ref[0]) * scale
    r = lax.broadcasted_iota(jnp.int32, (rows, 3 * BLOCK), 0)
    j = lax.broadcasted_iota(jnp.int32, (rows, 3 * BLOCK), 1)
    qpos = n * BLOCK + r % BLOCK
    kpos = (n - 1) * BLOCK + j
    rel = kpos - qpos
    ok = (rel <= WINDOW) & (rel >= -WINDOW) & (kpos >= 0) & (kpos < L) & (n < nb)
    s_loc = jnp.where(ok, s_loc, NEG)
    g_of_row = lax.broadcasted_iota(jnp.int32, (rows, 1), 0) // BLOCK
    sink = jnp.zeros((rows, 1), F32)
    for g in range(G):
        sink = jnp.where(g_of_row == g, sink_ref[h * G + g], sink)
    m = jnp.maximum(jnp.maximum(jnp.max(s_loc, axis=-1, keepdims=True),
                                jnp.max(s_ctx, axis=-1, keepdims=True)), sink)
    e_loc = jnp.exp(s_loc - m)
    e_ctx = jnp.exp(s_ctx - m)
    den = (jnp.sum(e_loc, axis=-1, keepdims=True) + jnp.sum(e_ctx, axis=-1, keepdims=True)
           + jnp.exp(sink - m))
    inv = 1.0 / den
    o = _dot((e_loc * inv).astype(v_loc.dtype), v_loc) + _dot((e_ctx * inv).astype(v_loc.dtype), vx_ref[0])
    o_ref[0] = jnp.concatenate([o[g * BLOCK:(g + 1) * BLOCK] for g in range(G)], axis=1).astype(o_ref.dtype)


def swa_attention(qk_b, rest, sink, B, S, L, C, HB, v_blk0):
    G = HB // HB_KV
    nb = L // BLOCK
    xblk = L // C
    assert L % C == 0

    def loc(off, cblk):
        return lambda b, h, n: (b, jnp.clip(jnp.where(n < nb, n + off, 0), 0, nb - 1), cblk + h)

    kspec = lambda off: pl.BlockSpec((1, BLOCK, DB), loc(off, HB))
    vspec = lambda off: pl.BlockSpec((1, BLOCK, DB), loc(off, v_blk0))
    return pl.pallas_call(
        functools.partial(_swa_kernel, L=L, nb=nb, G=G),
        grid=(B, HB_KV, S // BLOCK),
        in_specs=[pl.BlockSpec(memory_space=pltpu.SMEM),
                  pl.BlockSpec((1, BLOCK, G * DB), lambda b, h, n: (b, n, h)),
                  kspec(-1), kspec(0), kspec(1),
                  pl.BlockSpec((1, C, DB), lambda b, h, n: (b, xblk, HB + h)),
                  vspec(-1), vspec(0), vspec(1),
                  pl.BlockSpec((1, C, DB), lambda b, h, n: (b, xblk, v_blk0 + h))],
        out_specs=pl.BlockSpec((1, BLOCK, G * DB), lambda b, h, n: (b, n, h)),
        out_shape=jax.ShapeDtypeStruct((B, S, HB * DB), BF16),
        compiler_params=_cparams(("parallel", "parallel", "arbitrary")),
        name="swa_attention",
    )(sink, qk_b, qk_b, qk_b, qk_b, qk_b, rest, rest, rest, rest)


def _delta_prep_kernel(x_ref, w_ref, o_ref, xs_ref, *, L, rc, mode):
    S, tc = x_ref.shape[1], x_ref.shape[2]
    pad = CONV_K // 2
    halo = 8
    xs_ref[0:halo, :] = jnp.zeros((halo, tc), F32)
    xs_ref[halo + S:halo + S + halo, :] = jnp.zeros((halo, tc), F32)
    xs_ref[halo:halo + S, :] = x_ref[0].astype(F32)
    w = w_ref[...]
    for r0 in range(0, S, rc):
        t = r0 + lax.broadcasted_iota(jnp.int32, (rc, 1), 0)
        acc = jnp.zeros((rc, tc), F32)
        for j in range(CONV_K):
            sl = xs_ref[halo + r0 + j - pad:halo + r0 + j - pad + rc, :]
            src = t + (j - pad)
            if r0 + rc == L and j > pad:
                sl = jnp.where(src < L, sl, 0.0)
            if r0 == L and j < pad:
                sl = jnp.where(src >= L, sl, 0.0)
            acc = acc + sl * w[j:j + 1, :]
        y = _silu(acc)
        if mode != "v":
            parts = []
            for h in range(tc // DKC):
                yh = y[:, h * DKC:(h + 1) * DKC]
                yh = yh * lax.rsqrt(jnp.sum(yh * yh, axis=-1, keepdims=True) + RMS_EPS)
                parts.append(yh * (DKC ** -0.5) if mode == "q" else yh)
            y = jnp.concatenate(parts, axis=1) if len(parts) > 1 else parts[0]
        o_ref[0, r0:r0 + rc, :] = y


def delta_prep(rest, conv_w_l, B, S, L, MIX_W, col_blk0, tc, rc):
    outs = []
    nt = MIX_W // tc
    for pi, mode in enumerate(("q", "k", "v")):
        outs.append(pl.pallas_call(
            functools.partial(_delta_prep_kernel, L=L, rc=rc, mode=mode),
            grid=(B, nt),
            in_specs=[pl.BlockSpec((1, S, tc), lambda b, j, pi=pi: (b, 0, col_blk0 + pi * nt + j)),
                      pl.BlockSpec((CONV_K, tc), lambda b, j, pi=pi: (0, pi * nt + j))],
            out_specs=pl.BlockSpec((1, S, tc), lambda b, j: (b, 0, j)),
            out_shape=jax.ShapeDtypeStruct((B, S, MIX_W), F32),
            scratch_shapes=[pltpu.VMEM((S + 16, tc), F32)],
            compiler_params=_cparams(("parallel", "parallel")),
            name="delta_prep_" + mode,
        )(rest, conv_w_l))
    return outs


def _split16(x):
    hi = x.astype(BF16)
    return hi, (x - hi.astype(F32)).astype(BF16)


def _prod2(x, y):
    return _dot(jnp.concatenate(x, axis=1), jnp.concatenate([y[0], y[1], y[0], y[1]], axis=0))


def _unit_tri_inverse(a_list, ii, jj):
    same16 = (ii // 16) == (jj // 16)
    same32 = (ii // 32) == (jj // 32)
    eye = (ii == jj).astype(F32)
    each = lambda f, *ls: [f(*xs) for xs in zip(*ls)]
    d = each(lambda a: _split16(jnp.where(same16, a, 0.0)), a_list)
    l1 = each(lambda a: _split16(jnp.where(same32 & jnp.logical_not(same16), a, 0.0)), a_list)
    l2 = each(lambda a: _split16(jnp.where(same32, 0.0, a)), a_list)
    d2 = each(lambda x: _split16(_prod2(x, x)), d)
    d4 = each(lambda x: _split16(_prod2(x, x)), d2)
    d8 = each(lambda x: _split16(_prod2(x, x)), d4)
    t = each(lambda a: eye - jnp.where(same16, a, 0.0), a_list)
    for p in (d2, d4, d8):
        t = each(lambda tt, pp: tt + _prod2(_split16(tt), pp), t, p)
    for l in (l1, l2):
        ts = each(_split16, t)
        m = each(lambda tt, ll: _split16(_prod2(tt, ll)), ts, l)
        t = each(lambda tf, mm, tt: tf - _prod2(mm, tt), t, m, ts)
    return t


def _delta_scan_kernel(qf_ref, kf_ref, vf_ref, qb_ref, kb_ref, vb_ref, bacf_ref, bacb_ref, barf_ref, barb_ref,
                       pc_ref, pr_ref, of_ref, ob_ref, s_ref, *, HC):
    @pl.when(pl.program_id(1) == 0)
    def _():
        s_ref[...] = jnp.zeros_like(s_ref)

    ii = lax.broadcasted_iota(jnp.int32, (CHUNK, 2 * CHUNK), 0)
    jj = lax.broadcasted_iota(jnp.int32, (CHUNK, 2 * CHUNK), 1) % CHUNK
    dirs = ((qf_ref, kf_ref, vf_ref, bacf_ref, barf_ref, of_ref), (qb_ref, kb_ref, vb_ref, bacb_ref, barb_ref, ob_ref))
    chains = []
    for dn, (q_ref, k_ref, v_ref, bac_ref, bar_ref, o_ref) in enumerate(dirs):
        ahead = (ii - jj) if dn == 0 else (jj - ii)
        strict = ahead > 0
        incl = ahead >= 0
        tri2 = incl.astype(F32)
        tri = tri2[:, :CHUNK]
        bac = bac_ref[0, 0]
        pc = pc_ref[dn]
        beta_c = jax.nn.sigmoid(bac)
        g_c = -jnp.exp(pc[0:1]) * _softplus(bac + pc[1:2])
        gc_c = _dot(tri, g_c, HIGHEST)
        gtot_c = jnp.sum(g_c, axis=0, keepdims=True)
        bar = bar_ref[0, 0, 0]
        pr = pr_ref[dn]
        g_r = -jnp.exp(pr[:, 0:1]) * _softplus(bar + pr[:, 1:2])
        gc_r = _dot_nt(g_r, jnp.concatenate([tri, tri], axis=0), HIGHEST)
        for h in range(HC):
            chains.append(dict(dn=dn, h=h, q_ref=q_ref, k_ref=k_ref, v_ref=v_ref, o_ref=o_ref, strict=strict,
                               incl=incl, beta=beta_c[:, h:h + 1], gcol=gc_c[:, HC + h:HC + h + 1],
                               grow=gc_r[HC + h:HC + h + 1, :], gtot=gtot_c[:, HC + h:HC + h + 1]))

    sl = lambda ch: slice(ch["h"] * DKC, (ch["h"] + 1) * DKC)
    for ch in chains:
        ch["k"] = ch["k_ref"][0, :, sl(ch)]
        ch["k16"] = ch["k"].astype(BF16)
        ch["kb"] = ch["k"] * ch["beta"]
        ch["dec"] = jnp.exp(jnp.where(ch["incl"], ch["gcol"] - ch["grow"], NEG))
        ch["egc"] = jnp.exp(ch["gcol"])
    for ch in chains:
        k2 = jnp.concatenate([ch["k16"], ch["k16"]], axis=0)
        ch["a"] = _dot_nt(ch["kb"].astype(BF16), k2) * jnp.where(ch["strict"], ch["dec"], 0.0)
    t_list = _unit_tri_inverse([ch["a"] for ch in chains], ii, jj)
    for ch, t in zip(chains, t_list):
        v = ch["v_ref"][0, :, sl(ch)]
        rhs = jnp.concatenate([v * ch["beta"], ch["kb"] * ch["egc"]], axis=1).astype(BF16)
        ch["uw"] = _dot(t[:, :CHUNK].astype(BF16), rhs)
    for ch in chains:
        q = ch["q_ref"][0, :, sl(ch)]
        ch["qk"] = (_dot_nt(q.astype(BF16), ch["k16"]) * ch["dec"][:, :CHUNK]).astype(BF16)
        ch["state"] = s_ref[ch["dn"], ch["h"]]
        lhs = jnp.concatenate([ch["uw"][:, DKC:], q * ch["egc"]], axis=0).astype(BF16)
        ch["ws_qs"] = _dot(lhs, ch["state"].astype(BF16))
    for ch in chains:
        ch["u16"] = (ch["uw"][:, :DKC] - ch["ws_qs"][:CHUNK]).astype(BF16)
    for ch in chains:
        ch["o_ref"][0, :, sl(ch)] = ch["ws_qs"][CHUNK:] + _dot(ch["qk"], ch["u16"])
        k_tail = (ch["k"] * jnp.exp(ch["gtot"] - ch["gcol"])).astype(BF16)
        s_ref[ch["dn"], ch["h"]] = ch["state"] * jnp.exp(ch["gtot"]) + _dot_tn(k_tail, ch["u16"])


def delta_scan(qc, kc, vc, bac, bar, pc, pr, B, S, L, HC):
    nch = S // CHUNK
    nl = L // CHUNK
    fwd = lambda b, c: (b, (c + nl) % nch, 0)
    bwd = lambda b, c: (b, nch - 1 - c, 0)
    mw = HC * DKC
    row = lambda m: pl.BlockSpec((1, CHUNK, mw), m)
    bac_spec = lambda dn, m: pl.BlockSpec((1, 1, CHUNK, LANE), lambda b, c: (dn,) + m(b, c))
    bar_spec = lambda dn, m: pl.BlockSpec((1, 1, 1, 2 * HC, CHUNK), lambda b, c: (dn,) + m(b, c) + (0,))
    return pl.pallas_call(
        functools.partial(_delta_scan_kernel, HC=HC),
        grid=(B, nch),
        in_specs=[row(fwd), row(fwd), row(fwd), row(bwd), row(bwd), row(bwd),
                  bac_spec(0, fwd), bac_spec(1, bwd), bar_spec(0, fwd), bar_spec(1, bwd),
                  pl.BlockSpec((2, 8, LANE), lambda b, c: (0, 0, 0)),
                  pl.BlockSpec((2, 2 * HC, LANE), lambda b, c: (0, 0, 0))],
        out_specs=[row(fwd), row(bwd)],
        out_shape=[jax.ShapeDtypeStruct((B, S, mw), F32), jax.ShapeDtypeStruct((B, S, mw), F32)],
        scratch_shapes=[pltpu.VMEM((2, HC, DKC, DKC), F32)],
        compiler_params=_cparams(("parallel", "arbitrary")),
        name="delta_scan",
    )(qc, kc, vc, qc, kc, vc, bac, bac, bar, bar, pc, pr)


def _delta_post_kernel(of_ref, ob_ref, z_ref, nw_ref, y_ref):
    o = of_ref[...] + ob_ref[...]
    z = z_ref[...].astype(F32)
    nw = nw_ref[...]
    parts = []
    for h in range(o.shape[1] // DKC):
        oh = o[:, h * DKC:(h + 1) * DKC]
        oh = oh * lax.rsqrt(jnp.mean(oh * oh, axis=-1, keepdims=True) + RMS_EPS) * nw
        parts.append(oh * _silu(z[:, h * DKC:(h + 1) * DKC]))
    y_ref[...] = jnp.concatenate(parts, axis=1).astype(y_ref.dtype)


def delta_post(o_f, o_b, rest, norm_w, T, MIX_W, z_blk, tm):
    return pl.pallas_call(
        _delta_post_kernel,
        grid=(T // tm,),
        in_specs=[pl.BlockSpec((tm, MIX_W), lambda i: (i, 0)),
                  pl.BlockSpec((tm, MIX_W), lambda i: (i, 0)),
                  pl.BlockSpec((tm, MIX_W), lambda i: (i, z_blk)),
                  pl.BlockSpec((1, DKC), lambda i: (0, 0))],
        out_specs=pl.BlockSpec((tm, MIX_W), lambda i: (i, 0)),
        out_shape=jax.ShapeDtypeStruct((T, MIX_W), BF16),
        compiler_params=_cparams(("parallel",)),
        name="delta_post",
    )(o_f, o_b, rest, norm_w)


def _merge_kernel(oa_ref, ob_ref, oc_ref, w_ref, ga_ref, gb_ref, gc_ref, m_ref):
    acc = jax.nn.sigmoid(ga_ref[...].astype(F32)) * _dot(oa_ref[...], w_ref[0])
    acc = acc + jax.nn.sigmoid(gb_ref[...].astype(F32)) * _dot(ob_ref[...], w_ref[1])
    acc = acc + jax.nn.sigmoid(gc_ref[...].astype(F32)) * _dot(oc_ref[...], w_ref[2])
    m_ref[...] = acc.astype(m_ref.dtype)


def merge_branches(oa, ob, oc, w_branch, rest, T, D, MIX_W, gate_blk0, tm, tn):
    nd = D // tn
    o_spec = pl.BlockSpec((tm, MIX_W), lambda i, j: (i, 0))
    g_spec = lambda br: pl.BlockSpec((tm, tn), lambda i, j, br=br: (i, gate_blk0 + br * nd + j))
    return pl.pallas_call(
        _merge_kernel,
        grid=(T // tm, nd),
        in_specs=[o_spec, o_spec, o_spec,
                  pl.BlockSpec((3, MIX_W, tn), lambda i, j: (0, 0, j)),
                  g_spec(0), g_spec(1), g_spec(2)],
        out_specs=pl.BlockSpec((tm, tn), lambda i, j: (i, j)),
        out_shape=jax.ShapeDtypeStruct((T, D), BF16),
        compiler_params=_cparams(("parallel", "arbitrary")),
        name="merge_branches",
    )(oa, ob, oc, w_branch, rest, rest, rest)


def _store_packed(ref, y):
    n, d = y.shape
    groups = d // (2 * LANE)

    def rne_bits(v):
        b = lax.bitcast_convert_type(v, jnp.uint32)
        return b + jnp.uint32(0x7FFF) + ((b >> 16) & jnp.uint32(1))

    for g in range(groups):
        lo = y[:, 2 * LANE * g:2 * LANE * g + LANE]
        hi = y[:, 2 * LANE * g + LANE:2 * LANE * (g + 1)]
        ref[pl.ds(g, n, stride=groups), :] = (rne_bits(hi) & jnp.uint32(0xFFFF0000)) | (rne_bits(lo) >> 16)


def _load_packed(ref, n, groups, base=()):
    slabs = []
    for g in range(groups):
        w = ref[base + (pl.ds(g, n, stride=groups), slice(None))]
        slabs.append(lax.bitcast_convert_type(w << 16, F32))
        slabs.append(lax.bitcast_convert_type(w & jnp.uint32(0xFFFF0000), F32))
    return slabs


def _outproj_ln_kernel(m_ref, w_ref, x_ref, g_ref, b_ref, gl_ref, gc_ref, shl_ref, shc_ref, scl_ref, scc_ref,
                       x1_ref, a2_ref, a2p_ref, *, S, L, alpha):
    tm = x_ref.shape[0]
    ctx = _is_ctx_rows(pl.program_id(0), tm, S, L)
    gate = jnp.where(ctx, gc_ref[0, 0], gl_ref[0, 0])
    y = alpha * x_ref[...] + gate * _dot(m_ref[...], w_ref[...])
    x1 = _layer_norm(y, g_ref[...], b_ref[...])
    x1_ref[...] = x1
    sh = jnp.where(ctx, shc_ref[0, 0], shl_ref[0, 0])
    sc = jnp.where(ctx, scc_ref[0, 0], scl_ref[0, 0])
    a2 = x1 * (1.0 + sc) + sh
    a2_ref[...] = a2
    _store_packed(a2p_ref, a2)


def outproj_ln(merged, w_out, x, ln_g, ln_b, mod4, B, S, L, alpha, tm):
    T, D = x.shape
    row = pl.BlockSpec((tm, D), lambda i: (i, 0))
    vec = pl.BlockSpec((1, D), lambda i: (0, 0))
    return pl.pallas_call(
        functools.partial(_outproj_ln_kernel, S=S, L=L, alpha=alpha),
        grid=(T // tm,),
        in_specs=[row, pl.BlockSpec((D, D), lambda i: (0, 0)), row, vec, vec] + _mod_specs((2, 3, 4), tm, S, B, D),
        out_specs=[row, row, pl.BlockSpec((tm * (D // 256), LANE), lambda i: (i, 0))],
        out_shape=[jax.ShapeDtypeStruct((T, D), F32), jax.ShapeDtypeStruct((T, D), F32),
                   jax.ShapeDtypeStruct((T * (D // 256), LANE), jnp.uint32)],
        compiler_params=_cparams(("parallel",)),
        name="outproj_ln",
    )(merged, w_out, x, ln_g, ln_b, *([mod4] * 6))


def _router_kernel(a_ref, w_ref, bias_ref, gate_ref, idx_ref, *, E):
    a_hi, a_lo = _split16(a_ref[...])
    w_hi, w_lo = _split16(w_ref[...])
    logits = _dot(a_hi, w_hi) + (_dot(a_hi, w_lo) + _dot(a_lo, w_hi))
    scores = jax.nn.sigmoid(logits)
    lane = lax.broadcasted_iota(jnp.int32, scores.shape, 1)
    work = jnp.where(lane < E, scores + bias_ref[...], NEG)
    chosen = jnp.zeros(scores.shape, jnp.bool_)
    idx = jnp.zeros(scores.shape, jnp.int32)
    for kk in range(TOP_K):
        m = jnp.max(work, axis=-1, keepdims=True)
        first = jnp.min(jnp.where(work == m, lane, LANE), axis=-1, keepdims=True)
        pick = lane == first
        chosen = jnp.logical_or(chosen, pick)
        idx = jnp.where(lane == kk, first, idx)
        work = jnp.where(pick, 2.0 * NEG, work)
    sel = jnp.where(chosen, scores, 0.0)
    gate_ref[...] = sel / jnp.sum(sel, axis=-1, keepdims=True) * ROUTED_SCALE
    idx_ref[...] = idx


def router(a2, w_router_pad, bias_pad, E, tm):
    T, D = a2.shape
    return pl.pallas_call(
        functools.partial(_router_kernel, E=E),
        grid=(T // tm,),
        in_specs=[pl.BlockSpec((tm, D), lambda i: (i, 0)),
                  pl.BlockSpec((D, LANE), lambda i: (0, 0)),
                  pl.BlockSpec((1, LANE), lambda i: (0, 0))],
        out_specs=[pl.BlockSpec((tm, LANE), lambda i: (i, 0)), pl.BlockSpec((tm, LANE), lambda i: (i, 0))],
        out_shape=[jax.ShapeDtypeStruct((T, LANE), F32), jax.ShapeDtypeStruct((T, LANE), jnp.int32)],
        compiler_params=_cparams(("parallel",)),
        name="router",
    )(a2, w_router_pad, bias_pad)


def route_plan(idx8, gate, E, tme):
    T = idx8.shape[0]
    P = T * TOP_K
    assert T >= 3 * tme
    nt = -(-P // tme) + E + 1
    e_flat = idx8.reshape(P)
    sorted_p = jnp.sort(e_flat * P + jnp.arange(P, dtype=jnp.int32)) % P
    experts = jnp.arange(E, dtype=jnp.int32)
    counts = jnp.sum((e_flat[:, None] == experts[None, :]).astype(jnp.int32), axis=0)
    starts = jnp.cumsum(counts) - counts
    tiles_e = (counts + tme - 1) // tme
    tile_cum = jnp.cumsum(tiles_e)
    n_used = tile_cum[-1:].astype(jnp.int32)
    tile_ids = jnp.arange(nt, dtype=jnp.int32)
    te = jnp.minimum(jnp.sum((tile_ids[:, None] >= tile_cum[None, :]).astype(jnp.int32), axis=1), E - 1)
    local = tile_ids - (tile_cum - tiles_e)[te]
    lane = jnp.arange(tme, dtype=jnp.int32)[None, :]
    row_in_e = local[:, None] * tme + lane
    valid = (row_in_e < counts[te][:, None]) & (tile_ids[:, None] < n_used)
    pair = sorted_p[jnp.clip(starts[te][:, None] + row_in_e, 0, P - 1)]
    tok = jnp.where(valid, pair // TOP_K, 0)
    dummy = P + jnp.where(tile_ids < n_used, tile_ids % 2, 2)[:, None] * tme + lane
    dst = jnp.where(valid, (pair % TOP_K) * T + pair // TOP_K, dummy)
    wts = jnp.take_along_axis(gate, idx8, axis=1).reshape(P)
    w = jnp.where(valid, wts[pair], 0.0).reshape(nt * tme, 1)
    return te, n_used, tok.reshape(nt, 1, tme), dst.reshape(nt, 1, tme), w


def _experts_kernel(te_ref, nu_ref, tok0_ref, tokn_ref, dstp_ref, w_ref, wg_ref, wu_ref, wd_ref, x_hbm, y_hbm,
                    xb0, xb1, yb0, yb1, gsem, ssem, *, tme):
    i = pl.program_id(0)
    nu = nu_ref[0]
    xbufs, ybufs = (xb0, xb1), (yb0, yb1)
    G = xb0.shape[0] // tme

    def row_in(t, s, r):
        r0 = r * G if isinstance(r, int) else pl.multiple_of(r * G, G)
        return pltpu.make_async_copy(x_hbm.at[pl.ds(pl.multiple_of(t * G, G), G)], xbufs[s].at[pl.ds(r0, G)],
                                     gsem.at[s])

    def row_out(t, s, r):
        return pltpu.make_async_copy(ybufs[s].at[pl.ds(r * G, G)], y_hbm.at[pl.ds(pl.multiple_of(t * G, G), G)],
                                     ssem.at[s])

    def wait_in(s):
        pltpu.make_async_copy(x_hbm.at[pl.ds(0, tme * G)], xbufs[s], gsem.at[s]).wait()

    def wait_out(s):
        pltpu.make_async_copy(ybufs[s], y_hbm.at[pl.ds(0, tme * G)], ssem.at[s]).wait()

    @pl.when(i == 0)
    def _():
        yb1[...] = jnp.zeros_like(yb1)

        def body(r, carry):
            row_in(tok0_ref[0, 0, r], 0, r).start()
            return carry
        lax.fori_loop(0, tme, body, 0, unroll=8)

    def step(s, compute):
        wait_in(s)

        @pl.when(i >= 1)
        def _():
            wait_out(s)

        if compute:
            x16 = jnp.concatenate(_load_packed(xbufs[s], tme, G), axis=1).astype(BF16)
            hid = (_silu(_dot(x16, wg_ref[0])) * _dot(x16, wu_ref[0]) * w_ref[...]).astype(BF16)
            _store_packed(ybufs[s], _dot(hid, wd_ref[0]))
        for r in range(tme):
            row_out(dstp_ref[0, 0, r], 1 - s, r).start()
            if compute:
                row_in(tokn_ref[0, 0, r], 1 - s, r).start()
        if not compute:
            wait_out(1 - s)

    for s in range(2):
        @pl.when((i < nu) & (i % 2 == s))
        def _(s=s):
            step(s, True)

        @pl.when((i == nu) & (i % 2 == s))
        def _(s=s):
            step(s, False)


def experts_routed(a2p, plan, wg, wu, wd, tme):
    te, n_used, tok, dst, w = plan
    E, D, de = wg.shape
    G = D // (2 * LANE)
    T = a2p.shape[0] // G
    nt = tok.shape[0]
    P = T * TOP_K
    smem_tile = lambda m: pl.BlockSpec((1, 1, tme), m, memory_space=pltpu.SMEM)
    grid_spec = pltpu.PrefetchScalarGridSpec(
        num_scalar_prefetch=2,
        grid=(nt,),
        in_specs=[smem_tile(lambda i, te, nu: (0, 0, 0)),
                  smem_tile(lambda i, te, nu: (jnp.minimum(i + 1, nt - 1), 0, 0)),
                  smem_tile(lambda i, te, nu: (jnp.where(i == 0, nt - 1, i - 1), 0, 0)),
                  pl.BlockSpec((tme, 1), lambda i, te, nu: (i, 0)),
                  pl.BlockSpec((1, D, de), lambda i, te, nu: (te[i], 0, 0)),
                  pl.BlockSpec((1, D, de), lambda i, te, nu: (te[i], 0, 0)),
                  pl.BlockSpec((1, de, D), lambda i, te, nu: (te[i], 0, 0)),
                  pl.BlockSpec(memory_space=pl.ANY)],
        out_specs=pl.BlockSpec(memory_space=pl.ANY),
        scratch_shapes=[pltpu.VMEM((tme * G, LANE), jnp.uint32), pltpu.VMEM((tme * G, LANE), jnp.uint32),
                        pltpu.VMEM((tme * G, LANE), jnp.uint32), pltpu.VMEM((tme * G, LANE), jnp.uint32),
                        pltpu.SemaphoreType.DMA((2,)), pltpu.SemaphoreType.DMA((2,))])
    return pl.pallas_call(
        functools.partial(_experts_kernel, tme=tme),
        grid_spec=grid_spec,
        out_shape=jax.ShapeDtypeStruct(((P + T) * G, LANE), jnp.uint32),
        compiler_params=_cparams(("arbitrary",)),
        name="experts_routed",
    )(te, n_used, tok, tok, dst, w, wg, wu, wd, a2p)


def _shared_ln_kernel(a_ref, f_ref, x_ref, wgu_ref, wd_ref, g_ref, b_ref, gl_ref, gc_ref, shl_ref, shc_ref,
                      scl_ref, scc_ref, x2_ref, a1_ref, *, S, L, alpha, DS):
    tm = x_ref.shape[0]
    ctx = _is_ctx_rows(pl.program_id(0), tm, S, L)
    gu = _dot(a_ref[...].astype(BF16), wgu_ref[...])
    f = _dot((_silu(gu[:, :DS]) * gu[:, DS:]).astype(BF16), wd_ref[...])
    groups = f_ref.shape[1] // tm
    slabs = _load_packed(f_ref, tm, groups, base=(0,))
    for k in range(1, TOP_K):
        slabs = [a + b for a, b in zip(slabs, _load_packed(f_ref, tm, groups, base=(k,)))]
    f = f + jnp.concatenate(slabs, axis=1)
    gate = jnp.where(ctx, gc_ref[0, 0], gl_ref[0, 0])
    x2 = _layer_norm(alpha * x_ref[...] + gate * f, g_ref[...], b_ref[...])
    x2_ref[...] = x2
    sh = jnp.where(ctx, shc_ref[0, 0], shl_ref[0, 0])
    sc = jnp.where(ctx, scc_ref[0, 0], scl_ref[0, 0])
    a1_ref[...] = (x2 * (1.0 + sc) + sh).astype(a1_ref.dtype)


def shared_ln(a2, f_routed, x1, wgu_s, wd_s, ln_g, ln_b, mod4, mod4_next, B, S, L, alpha, tm):
    T, D = x1.shape
    ds2 = wgu_s.shape[1]
    row = pl.BlockSpec((tm, D), lambda i: (i, 0))
    vec = pl.BlockSpec((1, D), lambda i: (0, 0))
    return pl.pallas_call(
        functools.partial(_shared_ln_kernel, S=S, L=L, alpha=alpha, DS=ds2 // 2),
        grid=(T // tm,),
        in_specs=[row, pl.BlockSpec((TOP_K, tm * (D // 256), LANE), lambda i: (0, i, 0)), row,
                  pl.BlockSpec((D, ds2), lambda i: (0, 0)),
                  pl.BlockSpec((ds2 // 2, D), lambda i: (0, 0)),
                  vec, vec] + _mod_specs((5,), tm, S, B, D) + _mod_specs((0, 1), tm, S, B, D),
        out_specs=[row, row],
        out_shape=[jax.ShapeDtypeStruct((T, D), F32), jax.ShapeDtypeStruct((T, D), BF16)],
        compiler_params=_cparams(("parallel",)),
        name="shared_ln",
    )(a2, f_routed, x1, wgu_s, wd_s, ln_g, ln_b, mod4, mod4, *([mod4_next] * 4))


def _col_layout(MIX_W, D, HC):
    kv = (MIX_W, MIX_W, HB_KV * DB, HB_KV * DB, MIX_W, MIX_W, MIX_W, 2 * HC, 2 * HC)
    qs = (MIX_W, MIX_W, MIX_W, 3 * D)
    names = ("ak", "av", "bk", "bv", "cq", "ck", "cv", "cb", "ca", "qa", "qb", "z", "gates")
    offs, o = {}, 0
    for n, s in zip(names, kv + qs):
        offs[n] = (o, o + s)
        o += s
    return offs


def kernel(x, c, ctx, c_ctx, w_ada, b_ada, w_in, lam_q1, lam_k1, lam_q2, lam_k2, diff_norm_w, sink, conv_w, a_log, dt_bias, delta_norm_w, w_branch, w_out, ln1_g, ln1_b, ln2_g, ln2_b, w_router, router_bias, w_gate_e, w_up_e, w_down_e, w_gate_s, w_up_s, w_down_s):
    B, L, D = x.shape
    C = ctx.shape[1]
    S = L + C
    T = B * S
    depth = w_ada.shape[0]
    MIX_W = D // 2
    HA = MIX_W // (2 * DA)
    HB = MIX_W // DB
    HC = MIX_W // DKC
    E = w_router.shape[-1]
    alpha = (2 * depth) ** 0.25
    assert 2 * HC <= LANE and E <= LANE

    tm_tok = math.gcd(256, math.gcd(L, C))
    tm_mm = S // 3 if S % 3 == 0 and (S // 3) % 16 == 0 else tm_tok
    tn_mm = min(512, MIX_W)
    tq = tm_mm
    if S % 2 == 0 and (S // 2) % 16 == 0:
        tm_mm = S // 2
    tc = min(512, MIX_W)
    rc = math.gcd(256, math.gcd(L, C))
    tme = 256

    mb = -(-(B + 1) // 8) * 8
    c_all = jnp.zeros((mb, D), F32).at[:B].set(c).at[B].set(c_ctx)
    mods = compute_mods(c_all, w_ada, b_ada)
    mod4 = [mods[l].reshape(mb, 6, 1, D) for l in range(depth)]

    cos_a, sin_a, qt_a = rope_tables(L, C, DA, tn_mm)
    cos_b, sin_b, qt_b = rope_tables(L, C, DB, tn_mm if (HB + HB_KV) * DB % tn_mm == 0 else DB * HB_KV)
    tn_b = cos_b.shape[1]

    offs = _col_layout(MIX_W, D, HC)
    col = lambda w, n: w[:, offs[n][0]:offs[n][1]]
    n_rest = 5 * MIX_W + 3 * D + HB_KV * DB
    n_rest_pad = -(-n_rest // tn_mm) * tn_mm
    z_blk = 4
    gate_off = 5 * MIX_W
    bv_off = 5 * MIX_W + 3 * D

    h = jnp.concatenate([x, ctx], axis=1).reshape(T, D)
    a1 = modulate(h, mod4[0], B, S, L, tm_tok)

    for l in range(depth):
        wl = w_in[l].astype(BF16)
        w_qk_a = jnp.concatenate([col(wl, "qa"), col(wl, "ak")], axis=1)
        w_qk_b = jnp.concatenate([col(wl, "qb"), col(wl, "bk")], axis=1)
        w_rest = jnp.concatenate([col(wl, n) for n in ("av", "cq", "ck", "cv", "z", "gates", "bv")]
                                 + [jnp.zeros((D, n_rest_pad - n_rest), BF16)], axis=1)
        w_ba = jnp.concatenate([col(wl, "cb"), col(wl, "ca"), jnp.zeros((D, LANE - 4 * HC), BF16)], axis=1)

        qk_a = token_matmul(a1, w_qk_a, tm_mm, tn_mm, BF16, rope=(cos_a, sin_a, qt_a), S=S)
        qk_b = token_matmul(a1, w_qk_b, tm_mm, tn_b, BF16, rope=(cos_b, sin_b, qt_b), S=S)
        rest = token_matmul(a1, w_rest, tm_mm, tn_mm, BF16)
        ba = token_matmul(a1, w_ba, tm_mm, LANE, F32)

        lam_init = 0.8 - 0.6 * math.exp(-0.3 * l)
        lam = (jnp.exp(jnp.sum(lam_q1[l] * lam_k1[l])) - jnp.exp(jnp.sum(lam_q2[l] * lam_k2[l]))
               + lam_init).reshape(1).astype(F32)
        nw_a = (diff_norm_w[l] * (1.0 - lam_init)).reshape(1, 2 * DA)
        o_a = diff_attention(qk_a.reshape(B, S, -1), rest.reshape(B, S, -1), lam, nw_a, B, S, L, HA, tq)

        o_b = swa_attention(qk_b.reshape(B, S, -1), rest.reshape(B, S, -1), sink[l], B, S, L, C, HB,
                            bv_off // DB)

        qc, kc, vc = delta_prep(rest.reshape(B, S, -1), conv_w[l], B, S, L, MIX_W, MIX_W // tc, tc, rc)
        ba3 = ba.reshape(B, S, LANE)
        cb = ba3[..., :2 * HC].reshape(B, S, 2, HC)
        ca = ba3[..., 2 * HC:4 * HC].reshape(B, S, 2, HC)
        bdir = jnp.moveaxis(jnp.concatenate([cb, ca], axis=-1), 2, 0)
        bac = jnp.pad(bdir, ((0, 0), (0, 0), (0, 0), (0, LANE - 2 * HC)))
        bar = jnp.swapaxes(bdir.reshape(2, B, S // CHUNK, CHUNK, 2 * HC), -1, -2)
        pvec = jnp.stack([a_log[l], dt_bias[l]], axis=1).astype(F32)
        pc = jnp.zeros((2, 8, LANE), F32).at[:, :2, HC:2 * HC].set(pvec)
        pr = jnp.zeros((2, 2 * HC, LANE), F32).at[:, HC:, :2].set(jnp.swapaxes(pvec, 1, 2))
        o_cf, o_cb = delta_scan(qc, kc, vc, bac, bar, pc, pr, B, S, L, HC)
        o_c = delta_post(o_cf.reshape(T, MIX_W), o_cb.reshape(T, MIX_W), rest, delta_norm_w[l].reshape(1, DKC), T, MIX_W, z_blk,
                         tm_tok)

        merged = merge_branches(o_a.reshape(T, MIX_W), o_b.reshape(T, MIX_W), o_c, w_branch[l].astype(BF16),
                                rest, T, D, MIX_W, gate_off // tn_mm, tm_mm, tn_mm)
        x1, a2, a2p = outproj_ln(merged, w_out[l].astype(BF16), h, ln1_g[l].reshape(1, D), ln1_b[l].reshape(1, D),
                            mod4[l], B, S, L, alpha, tm_tok)

        w_r = jnp.pad(w_router[l], ((0, 0), (0, LANE - E)))
        b_r = jnp.pad(router_bias[l], (0, LANE - E)).reshape(1, LANE)
        gate, idx = router(a2, w_r, b_r, E, tm_tok)
        plan = route_plan(idx[:, :TOP_K], gate, E, tme)
        f_routed = experts_routed(a2p, plan, w_gate_e[l].astype(BF16), w_up_e[l].astype(BF16),
                                  w_down_e[l].astype(BF16), tme).reshape(TOP_K + 1, T * (D // 256), LANE)
        wgu_s = jnp.concatenate([w_gate_s[l], w_up_s[l]], axis=-1).astype(BF16)
        h, a1 = shared_ln(a2, f_routed, x1, wgu_s, w_down_s[l].astype(BF16), ln2_g[l].reshape(1, D),
                          ln2_b[l].reshape(1, D), mod4[l], mod4[min(l + 1, depth - 1)], B, S, L, alpha, tm_tok)

    return h.reshape(B, S, D)[:, :L]
```

```python
import functools
import math

import jax
import jax.numpy as jnp
from jax import lax
from jax.experimental import pallas as pl
from jax.experimental.pallas import tpu as pltpu

F32 = jnp.float32
BF16 = jnp.bfloat16
HIGHEST = lax.Precision.HIGHEST

LANE = 128
DA = 64
DB = 128
HB_KV = 2
DKC = 128
CONV_K = 5
CHUNK = 64
BLOCK = 128
WINDOW = 128
GRID_W = 64
ROPE_BASE = 10000.0
TOP_K = 8
ROUTED_SCALE = 2.5
LN_EPS = 1e-5
RMS_EPS = 1e-6
NEG = -1e30
VMEM_LIMIT = 56 * 1024 * 1024


def _cparams(sem):
    return pltpu.CompilerParams(dimension_semantics=sem, vmem_limit_bytes=VMEM_LIMIT)


def _dot(a, b, precision=None):
    return jnp.dot(a, b, preferred_element_type=F32, precision=precision)


def _dot_nt(a, b, precision=None):
    return lax.dot_general(a, b, (((1,), (1,)), ((), ())), preferred_element_type=F32, precision=precision)


def _dot_tn(a, b, precision=None):
    return lax.dot_general(a, b, (((0,), (0,)), ((), ())), preferred_element_type=F32, precision=precision)


def _silu(x):
    return x * jax.nn.sigmoid(x)


def _softplus(x):
    return jnp.maximum(x, 0.0) + jnp.log1p(jnp.exp(-jnp.abs(x)))


def _layer_norm(y, g, b):
    mu = jnp.mean(y, axis=-1, keepdims=True)
    yc = y - mu
    var = jnp.mean(yc * yc, axis=-1, keepdims=True)
    return yc * lax.rsqrt(var + LN_EPS) * g + b


def _pick(tile, S, L):
    assert S % tile == 0, (S, tile)
    return tile


def _is_ctx_rows(i, tm, S, L):
    row = (i * tm) % S + lax.broadcasted_iota(jnp.int32, (tm, 1), 0)
    return row >= L


def _mods_kernel(c_ref, w_ref, b_ref, o_ref):
    o_ref[0] = _dot(_silu(c_ref[...]), w_ref[0], HIGHEST) + b_ref[0]


def compute_mods(c_all, w_ada, b_ada):
    depth, D, n6 = w_ada.shape
    mb = c_all.shape[0]
    tn = 1024
    return pl.pallas_call(
        _mods_kernel,
        grid=(depth, n6 // tn),
        in_specs=[pl.BlockSpec((mb, D), lambda l, j: (0, 0)),
                  pl.BlockSpec((1, D, tn), lambda l, j: (l, 0, j)),
                  pl.BlockSpec((1, 1, tn), lambda l, j: (l, 0, j))],
        out_specs=pl.BlockSpec((1, mb, tn), lambda l, j: (l, 0, j)),
        out_shape=jax.ShapeDtypeStruct((depth, mb, n6), F32),
        compiler_params=_cparams(("arbitrary", "arbitrary")),
        name="mods",
    )(c_all, w_ada, b_ada.reshape(depth, 1, n6))


def _mod_specs(parts, tm, S, B, D):
    specs = []
    for p in parts:
        specs.append(pl.BlockSpec((1, 1, 1, D), lambda i, p=p: ((i * tm) // S, p, 0, 0)))
        specs.append(pl.BlockSpec((1, 1, 1, D), lambda i, p=p: (B, p, 0, 0)))
    return specs


def _modulate_kernel(x_ref, shl_ref, shc_ref, scl_ref, scc_ref, a_ref, *, S, L):
    tm = x_ref.shape[0]
    ctx = _is_ctx_rows(pl.program_id(0), tm, S, L)
    sh = jnp.where(ctx, shc_ref[0, 0], shl_ref[0, 0])
    sc = jnp.where(ctx, scc_ref[0, 0], scl_ref[0, 0])
    a_ref[...] = (x_ref[...] * (1.0 + sc) + sh).astype(a_ref.dtype)


def modulate(x, mod4, B, S, L, tm):
    T, D = x.shape
    return pl.pallas_call(
        functools.partial(_modulate_kernel, S=S, L=L),
        grid=(T // tm,),
        in_specs=[pl.BlockSpec((tm, D), lambda i: (i, 0))] + _mod_specs((0, 1), tm, S, B, D),
        out_specs=pl.BlockSpec((tm, D), lambda i: (i, 0)),
        out_shape=jax.ShapeDtypeStruct((T, D), BF16),
        compiler_params=_cparams(("parallel",)),
        name="modulate",
    )(x, mod4, mod4, mod4, mod4)


def _mm_kernel(a_ref, w_ref, *rest, quarter):
    o_ref = rest[-1]
    acc = _dot(a_ref[...], w_ref[...])
    if quarter:
        cos_ref, sin_ref = rest[0], rest[1]
        tn = acc.shape[1]
        lane = lax.broadcasted_iota(jnp.int32, acc.shape, 1)
        first = (lane % (2 * quarter)) < quarter
        rot = jnp.where(first, pltpu.roll(acc, tn - quarter, 1), pltpu.roll(acc, quarter, 1))
        acc = acc * cos_ref[...] + rot * sin_ref[...]
    o_ref[...] = acc.astype(o_ref.dtype)


def token_matmul(a, w, tm, tn, out_dtype, rope=None, S=None):
    T, K = a.shape
    N = w.shape[1]
    assert T % tm == 0 and N % tn == 0
    in_specs = [pl.BlockSpec((tm, K), lambda i, j: (i, 0)),
                pl.BlockSpec((K, tn), lambda i, j: (0, j))]
    args = [a, w]
    quarter = 0
    if rope is not None:
        cos, sin, quarter = rope
        nper = S // tm
        in_specs += [pl.BlockSpec((tm, tn), lambda i, j: (i % nper, 0)),
                     pl.BlockSpec((tm, tn), lambda i, j: (i % nper, 0))]
        args += [cos, sin]
    return pl.pallas_call(
        functools.partial(_mm_kernel, quarter=quarter),
        grid=(T // tm, N // tn),
        in_specs=in_specs,
        out_specs=pl.BlockSpec((tm, tn), lambda i, j: (i, j)),
        out_shape=jax.ShapeDtypeStruct((T, N), out_dtype),
        compiler_params=_cparams(("parallel", "arbitrary")),
        name="token_matmul",
    )(*args)


def rope_tables(L, C, d, tn):
    half, qt = d // 2, d // 4
    rows = L // GRID_W
    row = jnp.repeat(jnp.arange(rows), GRID_W)
    col = jnp.tile(jnp.arange(GRID_W), rows)
    inv = jnp.power(ROPE_BASE, -jnp.arange(0, half, 2, dtype=F32) / half)
    ang_r = row.astype(F32)[:, None] * inv[None, :]
    ang_c = col.astype(F32)[:, None] * inv[None, :]
    ang = jnp.concatenate([ang_r, ang_r, ang_c, ang_c], axis=-1)
    sign = jnp.where((jnp.arange(d) % half) < qt, -1.0, 1.0).astype(F32)
    cos = jnp.concatenate([jnp.cos(ang), jnp.ones((C, d), F32)], axis=0)
    sin = jnp.concatenate([jnp.sin(ang) * sign[None, :], jnp.zeros((C, d), F32)], axis=0)
    return jnp.tile(cos, (1, tn // d)), jnp.tile(sin, (1, tn // d)), qt


def _diff_attn_kernel(lam_ref, q_ref, k_ref, v_ref, nw_ref, o_ref, *, L, nlat):
    qi = pl.program_id(2)
    q = q_ref[0] * (DA ** -0.5)
    lane = lax.broadcasted_iota(jnp.int32, q.shape, 1)
    q1 = jnp.where(lane < DA, q, jnp.zeros_like(q))
    q2 = jnp.where(lane >= DA, q, jnp.zeros_like(q))

    k = k_ref[0]
    v = v_ref[0]
    tq, S = q.shape[0], k.shape[0]

    vx = jnp.concatenate([v, jnp.ones_like(v)], axis=1)

    def attend(valid):
        def one_map(qm):
            s = _dot_nt(qm, k)
            if valid is not None:
                s = jnp.where(valid, s, NEG)
            e = jnp.exp((s - jnp.max(s, axis=-1, keepdims=True)).astype(BF16))
            ol = _dot(e, vx)
            return ol[:, :LANE] / ol[:, LANE:]

        o = one_map(q1) - lam_ref[0] * one_map(q2)
        o = o * lax.rsqrt(jnp.mean(o * o, axis=-1, keepdims=True) + RMS_EPS) * nw_ref[...]
        o_ref[0] = o.astype(o_ref.dtype)

    @pl.when(qi < nlat)
    def _():
        attend(None)

    @pl.when(qi >= nlat)
    def _():
        row = qi * tq + lax.broadcasted_iota(jnp.int32, (tq, 1), 0)
        kidx = lax.broadcasted_iota(jnp.int32, (tq, S), 1)
        attend((kidx >= L) | (row < L))


def diff_attention(qk_a, rest, lam, norm_w_scaled, B, S, L, HA, tq):
    nlat = L // tq
    return pl.pallas_call(
        functools.partial(_diff_attn_kernel, L=L, nlat=nlat),
        grid=(B, HA, S // tq),
        in_specs=[pl.BlockSpec(memory_space=pltpu.SMEM),
                  pl.BlockSpec((1, tq, LANE), lambda b, h, i: (b, i, h)),
                  pl.BlockSpec((1, S, LANE), lambda b, h, i: (b, 0, HA + h)),
                  pl.BlockSpec((1, S, LANE), lambda b, h, i: (b, 0, h)),
                  pl.BlockSpec((1, LANE), lambda b, h, i: (0, 0))],
        out_specs=pl.BlockSpec((1, tq, LANE), lambda b, h, i: (b, i, h)),
        out_shape=jax.ShapeDtypeStruct((B, S, HA * LANE), BF16),
        compiler_params=_cparams(("parallel", "parallel", "arbitrary")),
        name="diff_attention",
    )(lam, qk_a, qk_a, rest, norm_w_scaled)


def _swa_kernel(sink_ref, q_ref, kp_ref, kc_ref, kn_ref, kx_ref, vp_ref, vc_ref, vn_ref, vx_ref, o_ref,
                *, L, nb, G):
    h = pl.program_id(1)
    n = pl.program_id(2)
    q = jnp.concatenate([q_ref[0, :, g * DB:(g + 1) * DB] for g in range(G)], axis=0)
    k_loc = jnp.concatenate([kp_ref[0], kc_ref[0], kn_ref[0]], axis=0)
    v_loc = jnp.concatenate([vp_ref[0], vc_ref[0], vn_ref[0]], axis=0)
    rows = G * BLOCK
    scale = DB ** -0.5
    s_loc = _dot_nt(q, k_loc) * scale
    s_ctx = _dot_nt(q, kx_ref[0]) * scale
    r = lax.broadcasted_iota(jnp.int32, (rows, 3 * BLOCK), 0)
    j = lax.broadcasted_iota(jnp.int32, (rows, 3 * BLOCK), 1)
    qpos = n * BLOCK + r % BLOCK
    kpos = (n - 1) * BLOCK + j
    rel = kpos - qpos
    ok = (rel <= WINDOW) & (rel >= -WINDOW) & (kpos >= 0) & (kpos < L) & (n < nb)
    s_loc = jnp.where(ok, s_loc, NEG)
    g_of_row = lax.broadcasted_iota(jnp.int32, (rows, 1), 0) // BLOCK
    sink = jnp.zeros((rows, 1), F32)
    for g in range(G):
        sink = jnp.where(g_of_row == g, sink_ref[h * G + g], sink)
    m = jnp.maximum(jnp.maximum(jnp.max(s_loc, axis=-1, keepdims=True),
                                jnp.max(s_ctx, axis=-1, keepdims=True)), sink)
    e_loc = jnp.exp(s_loc - m)
    e_ctx = jnp.exp(s_ctx - m)
    den = (jnp.sum(e_loc, axis=-1, keepdims=True) + jnp.sum(e_ctx, axis=-1, keepdims=True)
           + jnp.exp(sink - m))
    inv = 1.0 / den
    o = _dot((e_loc * inv).astype(v_loc.dtype), v_loc) + _dot((e_ctx * inv).astype(v_loc.dtype), vx_ref[0])
    o_ref[0] = jnp.concatenate([o[g * BLOCK:(g + 1) * BLOCK] for g in range(G)], axis=1).astype(o_ref.dtype)


def swa_attention(qk_b, rest, sink, B, S, L, C, HB, v_blk0):
    G = HB // HB_KV
    nb = L // BLOCK
    xblk = L // C
    assert L % C == 0

    def loc(off, cblk):
        return lambda b, h, n: (b, jnp.clip(jnp.where(n < nb, n + off, 0), 0, nb - 1), cblk + h)

    kspec = lambda off: pl.BlockSpec((1, BLOCK, DB), loc(off, HB))
    vspec = lambda off: pl.BlockSpec((1, BLOCK, DB), loc(off, v_blk0))
    return pl.pallas_call(
        functools.partial(_swa_kernel, L=L, nb=nb, G=G),
        grid=(B, HB_KV, S // BLOCK),
        in_specs=[pl.BlockSpec(memory_space=pltpu.SMEM),
                  pl.BlockSpec((1, BLOCK, G * DB), lambda b, h, n: (b, n, h)),
                  kspec(-1), kspec(0), kspec(1),
                  pl.BlockSpec((1, C, DB), lambda b, h, n: (b, xblk, HB + h)),
                  vspec(-1), vspec(0), vspec(1),
                  pl.BlockSpec((1, C, DB), lambda b, h, n: (b, xblk, v_blk0 + h))],
        out_specs=pl.BlockSpec((1, BLOCK, G * DB), lambda b, h, n: (b, n, h)),
        out_shape=jax.ShapeDtypeStruct((B, S, HB * DB), BF16),
        compiler_params=_cparams(("parallel", "parallel", "arbitrary")),
        name="swa_attention",
    )(sink, qk_b, qk_b, qk_b, qk_b, qk_b, rest, rest, rest, rest)


def _delta_prep_kernel(x_ref, w_ref, o_ref, xs_ref, *, L, rc, mode):
    S, tc = x_ref.shape[1], x_ref.shape[2]
    pad = CONV_K // 2
    halo = 8
    xs_ref[0:halo, :] = jnp.zeros((halo, tc), F32)
    xs_ref[halo + S:halo + S + halo, :] = jnp.zeros((halo, tc), F32)
    xs_ref[halo:halo + S, :] = x_ref[0].astype(F32)
    w = w_ref[...]
    for r0 in range(0, S, rc):
        t = r0 + lax.broadcasted_iota(jnp.int32, (rc, 1), 0)
        acc = jnp.zeros((rc, tc), F32)
        for j in range(CONV_K):
            sl = xs_ref[halo + r0 + j - pad:halo + r0 + j - pad + rc, :]
            src = t + (j - pad)
            if r0 + rc == L and j > pad:
                sl = jnp.where(src < L, sl, 0.0)
            if r0 == L and j < pad:
                sl = jnp.where(src >= L, sl, 0.0)
            acc = acc + sl * w[j:j + 1, :]
        y = _silu(acc)
        if mode != "v":
            parts = []
            for h in range(tc // DKC):
                yh = y[:, h * DKC:(h + 1) * DKC]
                yh = yh * lax.rsqrt(jnp.sum(yh * yh, axis=-1, keepdims=True) + RMS_EPS)
                parts.append(yh * (DKC ** -0.5) if mode == "q" else yh)
            y = jnp.concatenate(parts, axis=1) if len(parts) > 1 else parts[0]
        o_ref[0, r0:r0 + rc, :] = y


def delta_prep(rest, conv_w_l, B, S, L, MIX_W, col_blk0, tc, rc):
    outs = []
    nt = MIX_W // tc
    for pi, mode in enumerate(("q", "k", "v")):
        outs.append(pl.pallas_call(
            functools.partial(_delta_prep_kernel, L=L, rc=rc, mode=mode),
            grid=(B, nt),
            in_specs=[pl.BlockSpec((1, S, tc), lambda b, j, pi=pi: (b, 0, col_blk0 + pi * nt + j)),
                      pl.BlockSpec((CONV_K, tc), lambda b, j, pi=pi: (0, pi * nt + j))],
            out_specs=pl.BlockSpec((1, S, tc), lambda b, j: (b, 0, j)),
            out_shape=jax.ShapeDtypeStruct((B, S, MIX_W), F32),
            scratch_shapes=[pltpu.VMEM((S + 16, tc), F32)],
            compiler_params=_cparams(("parallel", "parallel")),
            name="delta_prep_" + mode,
        )(rest, conv_w_l))
    return outs


def _split16(x):
    hi = x.astype(BF16)
    return hi, (x - hi.astype(F32)).astype(BF16)


def _prod2(x, y):
    return _dot(jnp.concatenate(x, axis=1), jnp.concatenate([y[0], y[1], y[0], y[1]], axis=0))


def _unit_tri_inverse(a_list, ii, jj):
    same16 = (ii // 16) == (jj // 16)
    same32 = (ii // 32) == (jj // 32)
    eye = (ii == jj).astype(F32)
    each = lambda f, *ls: [f(*xs) for xs in zip(*ls)]
    d = each(lambda a: _split16(jnp.where(same16, a, 0.0)), a_list)
    l1 = each(lambda a: _split16(jnp.where(same32 & jnp.logical_not(same16), a, 0.0)), a_list)
    l2 = each(lambda a: _split16(jnp.where(same32, 0.0, a)), a_list)
    d2 = each(lambda x: _split16(_prod2(x, x)), d)
    d4 = each(lambda x: _split16(_prod2(x, x)), d2)
    d8 = each(lambda x: _split16(_prod2(x, x)), d4)
    t = each(lambda a: eye - jnp.where(same16, a, 0.0), a_list)
    for p in (d2, d4, d8):
        t = each(lambda tt, pp: tt + _prod2(_split16(tt), pp), t, p)
    for l in (l1, l2):
        ts = each(_split16, t)
        m = each(lambda tt, ll: _split16(_prod2(tt, ll)), ts, l)
        t = each(lambda tf, mm, tt: tf - _prod2(mm, tt), t, m, ts)
    return t


def _delta_scan_kernel(qf_ref, kf_ref, vf_ref, qb_ref, kb_ref, vb_ref, bacf_ref, bacb_ref, barf_ref, barb_ref,
                       pc_ref, pr_ref, of_ref, ob_ref, s_ref, *, HC):
    @pl.when(pl.program_id(1) == 0)
    def _():
        s_ref[...] = jnp.zeros_like(s_ref)

    ii = lax.broadcasted_iota(jnp.int32, (CHUNK, 2 * CHUNK), 0)
    jj = lax.broadcasted_iota(jnp.int32, (CHUNK, 2 * CHUNK), 1) % CHUNK
    dirs = ((qf_ref, kf_ref, vf_ref, bacf_ref, barf_ref, of_ref), (qb_ref, kb_ref, vb_ref, bacb_ref, barb_ref, ob_ref))
    chains = []
    for dn, (q_ref, k_ref, v_ref, bac_ref, bar_ref, o_ref) in enumerate(dirs):
        ahead = (ii - jj) if dn == 0 else (jj - ii)
        strict = ahead > 0
        incl = ahead >= 0
        tri2 = incl.astype(F32)
        tri = tri2[:, :CHUNK]
        bac = bac_ref[0, 0]
        pc = pc_ref[dn]
        beta_c = jax.nn.sigmoid(bac)
        g_c = -jnp.exp(pc[0:1]) * _softplus(bac + pc[1:2])
        gc_c = _dot(tri, g_c, HIGHEST)
        gtot_c = jnp.sum(g_c, axis=0, keepdims=True)
        bar = bar_ref[0, 0, 0]
        pr = pr_ref[dn]
        g_r = -jnp.exp(pr[:, 0:1]) * _softplus(bar + pr[:, 1:2])
        gc_r = _dot_nt(g_r, jnp.concatenate([tri, tri], axis=0), HIGHEST)
        for h in range(HC):
            chains.append(dict(dn=dn, h=h, q_ref=q_ref, k_ref=k_ref, v_ref=v_ref, o_ref=o_ref, strict=strict,
                               incl=incl, beta=beta_c[:, h:h + 1], gcol=gc_c[:, HC + h:HC + h + 1],
                               grow=gc_r[HC + h:HC + h + 1, :], gtot=gtot_c[:, HC + h:HC + h + 1]))

    sl = lambda ch: slice(ch["h"] * DKC, (ch["h"] + 1) * DKC)
    for ch in chains:
        ch["k"] = ch["k_ref"][0, :, sl(ch)]
        ch["k16"] = ch["k"].astype(BF16)
        ch["kb"] = ch["k"] * ch["beta"]
        ch["dec"] = jnp.exp(jnp.where(ch["incl"], ch["gcol"] - ch["grow"], NEG))
        ch["egc"] = jnp.exp(ch["gcol"])
    for ch in chains:
        k2 = jnp.concatenate([ch["k16"], ch["k16"]], axis=0)
        ch["a"] = _dot_nt(ch["kb"].astype(BF16), k2) * jnp.where(ch["strict"], ch["dec"], 0.0)
    t_list = _unit_tri_inverse([ch["a"] for ch in chains], ii, jj)
    for ch, t in zip(chains, t_list):
        v = ch["v_ref"][0, :, sl(ch)]
        rhs = jnp.concatenate([v * ch["beta"], ch["kb"] * ch["egc"]], axis=1).astype(BF16)
        ch["uw"] = _dot(t[:, :CHUNK].astype(BF16), rhs)
    for ch in chains:
        q = ch["q_ref"][0, :, sl(ch)]
        ch["qk"] = (_dot_nt(q.astype(BF16), ch["k16"]) * ch["dec"][:, :CHUNK]).astype(BF16)
        ch["state"] = s_ref[ch["dn"], ch["h"]]
        lhs = jnp.concatenate([ch["uw"][:, DKC:], q * ch["egc"]], axis=0).astype(BF16)
        ch["ws_qs"] = _dot(lhs, ch["state"].astype(BF16))
    for ch in chains:
        ch["u16"] = (ch["uw"][:, :DKC] - ch["ws_qs"][:CHUNK]).astype(BF16)
    for ch in chains:
        ch["o_ref"][0, :, sl(ch)] = ch["ws_qs"][CHUNK:] + _dot(ch["qk"], ch["u16"])
        k_tail = (ch["k"] * jnp.exp(ch["gtot"] - ch["gcol"])).astype(BF16)
        s_ref[ch["dn"], ch["h"]] = ch["state"] * jnp.exp(ch["gtot"]) + _dot_tn(k_tail, ch["u16"])


def delta_scan(qc, kc, vc, bac, bar, pc, pr, B, S, L, HC):
    nch = S // CHUNK
    nl = L // CHUNK
    fwd = lambda b, c: (b, (c + nl) % nch, 0)
    bwd = lambda b, c: (b, nch - 1 - c, 0)
    mw = HC * DKC
    row = lambda m: pl.BlockSpec((1, CHUNK, mw), m)
    bac_spec = lambda dn, m: pl.BlockSpec((1, 1, CHUNK, LANE), lambda b, c: (dn,) + m(b, c))
    bar_spec = lambda dn, m: pl.BlockSpec((1, 1, 1, 2 * HC, CHUNK), lambda b, c: (dn,) + m(b, c) + (0,))
    return pl.pallas_call(
        functools.partial(_delta_scan_kernel, HC=HC),
        grid=(B, nch),
        in_specs=[row(fwd), row(fwd), row(fwd), row(bwd), row(bwd), row(bwd),
                  bac_spec(0, fwd), bac_spec(1, bwd), bar_spec(0, fwd), bar_spec(1, bwd),
                  pl.BlockSpec((2, 8, LANE), lambda b, c: (0, 0, 0)),
                  pl.BlockSpec((2, 2 * HC, LANE), lambda b, c: (0, 0, 0))],
        out_specs=[row(fwd), row(bwd)],
        out_shape=[jax.ShapeDtypeStruct((B, S, mw), F32), jax.ShapeDtypeStruct((B, S, mw), F32)],
        scratch_shapes=[pltpu.VMEM((2, HC, DKC, DKC), F32)],
        compiler_params=_cparams(("parallel", "arbitrary")),
        name="delta_scan",
    )(qc, kc, vc, qc, kc, vc, bac, bac, bar, bar, pc, pr)


def _delta_post_kernel(of_ref, ob_ref, z_ref, nw_ref, y_ref):
    o = of_ref[...] + ob_ref[...]
    z = z_ref[...].astype(F32)
    nw = nw_ref[...]
    parts = []
    for h in range(o.shape[1] // DKC):
        oh = o[:, h * DKC:(h + 1) * DKC]
        oh = oh * lax.rsqrt(jnp.mean(oh * oh, axis=-1, keepdims=True) + RMS_EPS) * nw
        parts.append(oh * _silu(z[:, h * DKC:(h + 1) * DKC]))
    y_ref[...] = jnp.concatenate(parts, axis=1).astype(y_ref.dtype)


def delta_post(o_f, o_b, rest, norm_w, T, MIX_W, z_blk, tm):
    return pl.pallas_call(
        _delta_post_kernel,
        grid=(T // tm,),
        in_specs=[pl.BlockSpec((tm, MIX_W), lambda i: (i, 0)),
                  pl.BlockSpec((tm, MIX_W), lambda i: (i, 0)),
                  pl.BlockSpec((tm, MIX_W), lambda i: (i, z_blk)),
                  pl.BlockSpec((1, DKC), lambda i: (0, 0))],
        out_specs=pl.BlockSpec((tm, MIX_W), lambda i: (i, 0)),
        out_shape=jax.ShapeDtypeStruct((T, MIX_W), BF16),
        compiler_params=_cparams(("parallel",)),
        name="delta_post",
    )(o_f, o_b, rest, norm_w)


def _merge_kernel(oa_ref, ob_ref, oc_ref, w_ref, ga_ref, gb_ref, gc_ref, m_ref):
    acc = jax.nn.sigmoid(ga_ref[...].astype(F32)) * _dot(oa_ref[...], w_ref[0])
    acc = acc + jax.nn.sigmoid(gb_ref[...].astype(F32)) * _dot(ob_ref[...], w_ref[1])
    acc = acc + jax.nn.sigmoid(gc_ref[...].astype(F32)) * _dot(oc_ref[...], w_ref[2])
    m_ref[...] = acc.astype(m_ref.dtype)


def merge_branches(oa, ob, oc, w_branch, rest, T, D, MIX_W, gate_blk0, tm, tn):
    nd = D // tn
    o_spec = pl.BlockSpec((tm, MIX_W), lambda i, j: (i, 0))
    g_spec = lambda br: pl.BlockSpec((tm, tn), lambda i, j, br=br: (i, gate_blk0 + br * nd + j))
    return pl.pallas_call(
        _merge_kernel,
        grid=(T // tm, nd),
        in_specs=[o_spec, o_spec, o_spec,
                  pl.BlockSpec((3, MIX_W, tn), lambda i, j: (0, 0, j)),
                  g_spec(0), g_spec(1), g_spec(2)],
        out_specs=pl.BlockSpec((tm, tn), lambda i, j: (i, j)),
        out_shape=jax.ShapeDtypeStruct((T, D), BF16),
        compiler_params=_cparams(("parallel", "arbitrary")),
        name="merge_branches",
    )(oa, ob, oc, w_branch, rest, rest, rest)


def _store_packed(ref, y):
    n, d = y.shape
    groups = d // (2 * LANE)

    def rne_bits(v):
        b = lax.bitcast_convert_type(v, jnp.uint32)
        return b + jnp.uint32(0x7FFF) + ((b >> 16) & jnp.uint32(1))

    for g in range(groups):
        lo = y[:, 2 * LANE * g:2 * LANE * g + LANE]
        hi = y[:, 2 * LANE * g + LANE:2 * LANE * (g + 1)]
        ref[pl.ds(g, n, stride=groups), :] = (rne_bits(hi) & jnp.uint32(0xFFFF0000)) | (rne_bits(lo) >> 16)


def _load_packed(ref, n, groups, base=()):
    slabs = []
    for g in range(groups):
        w = ref[base + (pl.ds(g, n, stride=groups), slice(None))]
        slabs.append(lax.bitcast_convert_type(w << 16, F32))
        slabs.append(lax.bitcast_convert_type(w & jnp.uint32(0xFFFF0000), F32))
    return slabs


def _outproj_ln_kernel(m_ref, w_ref, x_ref, g_ref, b_ref, gl_ref, gc_ref, shl_ref, shc_ref, scl_ref, scc_ref,
                       x1_ref, a2_ref, a2p_ref, *, S, L, alpha):
    tm = x_ref.shape[0]
    ctx = _is_ctx_rows(pl.program_id(0), tm, S, L)
    gate = jnp.where(ctx, gc_ref[0, 0], gl_ref[0, 0])
    y = alpha * x_ref[...] + gate * _dot(m_ref[...], w_ref[...])
    x1 = _layer_norm(y, g_ref[...], b_ref[...])
    x1_ref[...] = x1
    sh = jnp.where(ctx, shc_ref[0, 0], shl_ref[0, 0])
    sc = jnp.where(ctx, scc_ref[0, 0], scl_ref[0, 0])
    a2 = x1 * (1.0 + sc) + sh
    a2_ref[...] = a2
    _store_packed(a2p_ref, a2)


def outproj_ln(merged, w_out, x, ln_g, ln_b, mod4, B, S, L, alpha, tm):
    T, D = x.shape
    row = pl.BlockSpec((tm, D), lambda i: (i, 0))
    vec = pl.BlockSpec((1, D), lambda i: (0, 0))
    return pl.pallas_call(
        functools.partial(_outproj_ln_kernel, S=S, L=L, alpha=alpha),
        grid=(T // tm,),
        in_specs=[row, pl.BlockSpec((D, D), lambda i: (0, 0)), row, vec, vec] + _mod_specs((2, 3, 4), tm, S, B, D),
        out_specs=[row, row, pl.BlockSpec((tm * (D // 256), LANE), lambda i: (i, 0))],
        out_shape=[jax.ShapeDtypeStruct((T, D), F32), jax.ShapeDtypeStruct((T, D), F32),
                   jax.ShapeDtypeStruct((T * (D // 256), LANE), jnp.uint32)],
        compiler_params=_cparams(("parallel",)),
        name="outproj_ln",
    )(merged, w_out, x, ln_g, ln_b, *([mod4] * 6))


def _router_kernel(a_ref, w_ref, bias_ref, gate_ref, idx_ref, *, E):
    a_hi, a_lo = _split16(a_ref[...])
    w_hi, w_lo = _split16(w_ref[...])
    logits = _dot(a_hi, w_hi) + (_dot(a_hi, w_lo) + _dot(a_lo, w_hi))
    scores = jax.nn.sigmoid(logits)
    lane = lax.broadcasted_iota(jnp.int32, scores.shape, 1)
    work = jnp.where(lane < E, scores + bias_ref[...], NEG)
    chosen = jnp.zeros(scores.shape, jnp.bool_)
    idx = jnp.zeros(scores.shape, jnp.int32)
    for kk in range(TOP_K):
        m = jnp.max(work, axis=-1, keepdims=True)
        first = jnp.min(jnp.where(work == m, lane, LANE), axis=-1, keepdims=True)
        pick = lane == first
        chosen = jnp.logical_or(chosen, pick)
        idx = jnp.where(lane == kk, first, idx)
        work = jnp.where(pick, 2.0 * NEG, work)
    sel = jnp.where(chosen, scores, 0.0)
    gate_ref[...] = sel / jnp.sum(sel, axis=-1, keepdims=True) * ROUTED_SCALE
    idx_ref[...] = idx


def router(a2, w_router_pad, bias_pad, E, tm):
    T, D = a2.shape
    return pl.pallas_call(
        functools.partial(_router_kernel, E=E),
        grid=(T // tm,),
        in_specs=[pl.BlockSpec((tm, D), lambda i: (i, 0)),
                  pl.BlockSpec((D, LANE), lambda i: (0, 0)),
                  pl.BlockSpec((1, LANE), lambda i: (0, 0))],
        out_specs=[pl.BlockSpec((tm, LANE), lambda i: (i, 0)), pl.BlockSpec((tm, LANE), lambda i: (i, 0))],
        out_shape=[jax.ShapeDtypeStruct((T, LANE), F32), jax.ShapeDtypeStruct((T, LANE), jnp.int32)],
        compiler_params=_cparams(("parallel",)),
        name="router",
    )(a2, w_router_pad, bias_pad)


def route_plan(idx8, gate, E, tme):
    T = idx8.shape[0]
    P = T * TOP_K
    assert T >= 3 * tme
    nt = -(-P // tme) + E + 1
    e_flat = idx8.reshape(P)
    sorted_p = jnp.sort(e_flat * P + jnp.arange(P, dtype=jnp.int32)) % P
    experts = jnp.arange(E, dtype=jnp.int32)
    counts = jnp.sum((e_flat[:, None] == experts[None, :]).astype(jnp.int32), axis=0)
    starts = jnp.cumsum(counts) - counts
    tiles_e = (counts + tme - 1) // tme
    tile_cum = jnp.cumsum(tiles_e)
    n_used = tile_cum[-1:].astype(jnp.int32)
    tile_ids = jnp.arange(nt, dtype=jnp.int32)
    te = jnp.minimum(jnp.sum((tile_ids[:, None] >= tile_cum[None, :]).astype(jnp.int32), axis=1), E - 1)
    local = tile_ids - (tile_cum - tiles_e)[te]
    lane = jnp.arange(tme, dtype=jnp.int32)[None, :]
    row_in_e = local[:, None] * tme + lane
    valid = (row_in_e < counts[te][:, None]) & (tile_ids[:, None] < n_used)
    pair = sorted_p[jnp.clip(starts[te][:, None] + row_in_e, 0, P - 1)]
    tok = jnp.where(valid, pair // TOP_K, 0)
    dummy = P + jnp.where(tile_ids < n_used, tile_ids % 2, 2)[:, None] * tme + lane
    dst = jnp.where(valid, (pair % TOP_K) * T + pair // TOP_K, dummy)
    wts = jnp.take_along_axis(gate, idx8, axis=1).reshape(P)
    w = jnp.where(valid, wts[pair], 0.0).reshape(nt * tme, 1)
    return te, n_used, tok.reshape(nt, 1, tme), dst.reshape(nt, 1, tme), w


def _experts_kernel(te_ref, nu_ref, tok0_ref, tok1_ref, tokn_ref, dstp_ref, w_ref, wg_ref, wu_ref, wd_ref, x_hbm,
                    y_hbm, xb0, xb1, xb2, yb0, yb1, yb2, gsem, ssem, *, tme):
    i = pl.program_id(0)
    nu = nu_ref[0]
    xbufs, ybufs = (xb0, xb1, xb2), (yb0, yb1, yb2)
    G = xb0.shape[0] // tme

    def row_in(t, s, r):
        r0 = r * G if isinstance(r, int) else pl.multiple_of(r * G, G)
        return pltpu.make_async_copy(x_hbm.at[pl.ds(pl.multiple_of(t * G, G), G)], xbufs[s].at[pl.ds(r0, G)],
                                     gsem.at[s])

    def row_out(t, s, r):
        return pltpu.make_async_copy(ybufs[s].at[pl.ds(r * G, G)], y_hbm.at[pl.ds(pl.multiple_of(t * G, G), G)],
                                     ssem.at[s])

    def wait_in(s):
        pltpu.make_async_copy(x_hbm.at[pl.ds(0, tme * G)], xbufs[s], gsem.at[s]).wait()

    def wait_out(s):
        pltpu.make_async_copy(ybufs[s], y_hbm.at[pl.ds(0, tme * G)], ssem.at[s]).wait()

    @pl.when(i == 0)
    def _():
        yb2[...] = jnp.zeros_like(yb2)

        def body(r, carry):
            row_in(tok0_ref[0, 0, r], 0, r).start()
            row_in(tok1_ref[0, 0, r], 1, r).start()
            return carry
        lax.fori_loop(0, tme, body, 0, unroll=8)

    def step(s, compute):
        prev, nxt = (s + 2) % 3, (s + 1) % 3
        wait_in(s)

        @pl.when(i >= 2)
        def _():
            wait_out(s)

        if compute:
            x16 = jnp.concatenate(_load_packed(xbufs[s], tme, G), axis=1).astype(BF16)
            hid = (_silu(_dot(x16, wg_ref[0])) * _dot(x16, wu_ref[0]) * w_ref[...]).astype(BF16)
            _store_packed(ybufs[s], _dot(hid, wd_ref[0]))
        for r in range(tme):
            row_out(dstp_ref[0, 0, r], prev, r).start()
            if compute:
                row_in(tokn_ref[0, 0, r], prev, r).start()
        if not compute:
            wait_out(prev)
            wait_out(nxt)
            wait_in(nxt)

    for s in range(3):
        @pl.when((i < nu) & (i % 3 == s))
        def _(s=s):
            step(s, True)

        @pl.when((i == nu) & (i % 3 == s))
        def _(s=s):
            step(s, False)


def experts_routed(a2p, plan, wg, wu, wd, tme):
    te, n_used, tok, dst, w = plan
    E, D, de = wg.shape
    G = D // (2 * LANE)
    T = a2p.shape[0] // G
    nt = tok.shape[0]
    P = T * TOP_K
    smem_tile = lambda m: pl.BlockSpec((1, 1, tme), m, memory_space=pltpu.SMEM)
    grid_spec = pltpu.PrefetchScalarGridSpec(
        num_scalar_prefetch=2,
        grid=(nt,),
        in_specs=[smem_tile(lambda i, te, nu: (0, 0, 0)),
                  smem_tile(lambda i, te, nu: (1, 0, 0)),
                  smem_tile(lambda i, te, nu: (jnp.minimum(i + 2, nt - 1), 0, 0)),
                  smem_tile(lambda i, te, nu: (jnp.where(i == 0, nt - 1, i - 1), 0, 0)),
                  pl.BlockSpec((tme, 1), lambda i, te, nu: (i, 0)),
                  pl.BlockSpec((1, D, de), lambda i, te, nu: (te[i], 0, 0)),
                  pl.BlockSpec((1, D, de), lambda i, te, nu: (te[i], 0, 0)),
                  pl.BlockSpec((1, de, D), lambda i, te, nu: (te[i], 0, 0)),
                  pl.BlockSpec(memory_space=pl.ANY)],
        out_specs=pl.BlockSpec(memory_space=pl.ANY),
        scratch_shapes=[pltpu.VMEM((tme * G, LANE), jnp.uint32)] * 6
        + [pltpu.SemaphoreType.DMA((3,)), pltpu.SemaphoreType.DMA((3,))])
    return pl.pallas_call(
        functools.partial(_experts_kernel, tme=tme),
        grid_spec=grid_spec,
        out_shape=jax.ShapeDtypeStruct(((P + T) * G, LANE), jnp.uint32),
        compiler_params=_cparams(("arbitrary",)),
        name="experts_routed",
    )(te, n_used, tok, tok, tok, dst, w, wg, wu, wd, a2p)


def _shared_ln_kernel(a_ref, f_ref, x_ref, wgu_ref, wd_ref, g_ref, b_ref, gl_ref, gc_ref, shl_ref, shc_ref,
                      scl_ref, scc_ref, x2_ref, a1_ref, *, S, L, alpha, DS):
    tm = x_ref.shape[0]
    ctx = _is_ctx_rows(pl.program_id(0), tm, S, L)
    gu = _dot(a_ref[...].astype(BF16), wgu_ref[...])
    f = _dot((_silu(gu[:, :DS]) * gu[:, DS:]).astype(BF16), wd_ref[...])
    groups = f_ref.shape[1] // tm
    slabs = _load_packed(f_ref, tm, groups, base=(0,))
    for k in range(1, TOP_K):
        slabs = [a + b for a, b in zip(slabs, _load_packed(f_ref, tm, groups, base=(k,)))]
    f = f + jnp.concatenate(slabs, axis=1)
    gate = jnp.where(ctx, gc_ref[0, 0], gl_ref[0, 0])
    x2 = _layer_norm(alpha * x_ref[...] + gate * f, g_ref[...], b_ref[...])
    x2_ref[...] = x2
    sh = jnp.where(ctx, shc_ref[0, 0], shl_ref[0, 0])
    sc = jnp.where(ctx, scc_ref[0, 0], scl_ref[0, 0])
    a1_ref[...] = (x2 * (1.0 + sc) + sh).astype(a1_ref.dtype)


def shared_ln(a2, f_routed, x1, wgu_s, wd_s, ln_g, ln_b, mod4, mod4_next, B, S, L, alpha, tm):
    T, D = x1.shape
    ds2 = wgu_s.shape[1]
    row = pl.BlockSpec((tm, D), lambda i: (i, 0))
    vec = pl.BlockSpec((1, D), lambda i: (0, 0))
    return pl.pallas_call(
        functools.partial(_shared_ln_kernel, S=S, L=L, alpha=alpha, DS=ds2 // 2),
        grid=(T // tm,),
        in_specs=[row, pl.BlockSpec((TOP_K, tm * (D // 256), LANE), lambda i: (0, i, 0)), row,
                  pl.BlockSpec((D, ds2), lambda i: (0, 0)),
                  pl.BlockSpec((ds2 // 2, D), lambda i: (0, 0)),
                  vec, vec] + _mod_specs((5,), tm, S, B, D) + _mod_specs((0, 1), tm, S, B, D),
        out_specs=[row, row],
        out_shape=[jax.ShapeDtypeStruct((T, D), F32), jax.ShapeDtypeStruct((T, D), BF16)],
        compiler_params=_cparams(("parallel",)),
        name="shared_ln",
    )(a2, f_routed, x1, wgu_s, wd_s, ln_g, ln_b, mod4, mod4, *([mod4_next] * 4))


def _col_layout(MIX_W, D, HC):
    kv = (MIX_W, MIX_W, HB_KV * DB, HB_KV * DB, MIX_W, MIX_W, MIX_W, 2 * HC, 2 * HC)
    qs = (MIX_W, MIX_W, MIX_W, 3 * D)
    names = ("ak", "av", "bk", "bv", "cq", "ck", "cv", "cb", "ca", "qa", "qb", "z", "gates")
    offs, o = {}, 0
    for n, s in zip(names, kv + qs):
        offs[n] = (o, o + s)
        o += s
    return offs


def kernel(x, c, ctx, c_ctx, w_ada, b_ada, w_in, lam_q1, lam_k1, lam_q2, lam_k2, diff_norm_w, sink, conv_w, a_log, dt_bias, delta_norm_w, w_branch, w_out, ln1_g, ln1_b, ln2_g, ln2_b, w_router, router_bias, w_gate_e, w_up_e, w_down_e, w_gate_s, w_up_s, w_down_s):
    B, L, D = x.shape
    C = ctx.shape[1]
    S = L + C
    T = B * S
    depth = w_ada.shape[0]
    MIX_W = D // 2
    HA = MIX_W // (2 * DA)
    HB = MIX_W // DB
    HC = MIX_W // DKC
    E = w_router.shape[-1]
    alpha = (2 * depth) ** 0.25
    assert 2 * HC <= LANE and E <= LANE

    tm_tok = math.gcd(256, math.gcd(L, C))
    tm_mm = S // 3 if S % 3 == 0 and (S // 3) % 16 == 0 else tm_tok
    tn_mm = min(512, MIX_W)
    tq = tm_mm
    if S % 2 == 0 and (S // 2) % 16 == 0:
        tm_mm = S // 2
    tc = min(512, MIX_W)
    rc = math.gcd(256, math.gcd(L, C))
    tme = 256

    mb = -(-(B + 1) // 8) * 8
    c_all = jnp.zeros((mb, D), F32).at[:B].set(c).at[B].set(c_ctx)
    mods = compute_mods(c_all, w_ada, b_ada)
    mod4 = [mods[l].reshape(mb, 6, 1, D) for l in range(depth)]

    cos_a, sin_a, qt_a = rope_tables(L, C, DA, tn_mm)
    cos_b, sin_b, qt_b = rope_tables(L, C, DB, tn_mm if (HB + HB_KV) * DB % tn_mm == 0 else DB * HB_KV)
    tn_b = cos_b.shape[1]

    offs = _col_layout(MIX_W, D, HC)
    col = lambda w, n: w[:, offs[n][0]:offs[n][1]]
    n_rest = 5 * MIX_W + 3 * D + HB_KV * DB
    n_rest_pad = -(-n_rest // tn_mm) * tn_mm
    z_blk = 4
    gate_off = 5 * MIX_W
    bv_off = 5 * MIX_W + 3 * D

    h = jnp.concatenate([x, ctx], axis=1).reshape(T, D)
    a1 = modulate(h, mod4[0], B, S, L, tm_tok)

    for l in range(depth):
        wl = w_in[l].astype(BF16)
        w_qk_a = jnp.concatenate([col(wl, "qa"), col(wl, "ak")], axis=1)
        w_qk_b = jnp.concatenate([col(wl, "qb"), col(wl, "bk")], axis=1)
        w_rest = jnp.concatenate([col(wl, n) for n in ("av", "cq", "ck", "cv", "z", "gates", "bv")]
                                 + [jnp.zeros((D, n_rest_pad - n_rest), BF16)], axis=1)
        w_ba = jnp.concatenate([col(wl, "cb"), col(wl, "ca"), jnp.zeros((D, LANE - 4 * HC), BF16)], axis=1)

        qk_a = token_matmul(a1, w_qk_a, tm_mm, tn_mm, BF16, rope=(cos_a, sin_a, qt_a), S=S)
        qk_b = token_matmul(a1, w_qk_b, tm_mm, tn_b, BF16, rope=(cos_b, sin_b, qt_b), S=S)
        rest = token_matmul(a1, w_rest, tm_mm, tn_mm, BF16)
        ba = token_matmul(a1, w_ba, tm_mm, LANE, F32)

        lam_init = 0.8 - 0.6 * math.exp(-0.3 * l)
        lam = (jnp.exp(jnp.sum(lam_q1[l] * lam_k1[l])) - jnp.exp(jnp.sum(lam_q2[l] * lam_k2[l]))
               + lam_init).reshape(1).astype(F32)
        nw_a = (diff_norm_w[l] * (1.0 - lam_init)).reshape(1, 2 * DA)
        o_a = diff_attention(qk_a.reshape(B, S, -1), rest.reshape(B, S, -1), lam, nw_a, B, S, L, HA, tq)

        o_b = swa_attention(qk_b.reshape(B, S, -1), rest.reshape(B, S, -1), sink[l], B, S, L, C, HB,
                            bv_off // DB)

        qc, kc, vc = delta_prep(rest.reshape(B, S, -1), conv_w[l], B, S, L, MIX_W, MIX_W // tc, tc, rc)
        ba3 = ba.reshape(B, S, LANE)
        cb = ba3[..., :2 * HC].reshape(B, S, 2, HC)
        ca = ba3[..., 2 * HC:4 * HC].reshape(B, S, 2, HC)
        bdir = jnp.moveaxis(jnp.concatenate([cb, ca], axis=-1), 2, 0)
        bac = jnp.pad(bdir, ((0, 0), (0, 0), (0, 0), (0, LANE - 2 * HC)))
        bar = jnp.swapaxes(bdir.reshape(2, B, S // CHUNK, CHUNK, 2 * HC), -1, -2)
        pvec = jnp.stack([a_log[l], dt_bias[l]], axis=1).astype(F32)
        pc = jnp.zeros((2, 8, LANE), F32).at[:, :2, HC:2 * HC].set(pvec)
        pr = jnp.zeros((2, 2 * HC, LANE), F32).at[:, HC:, :2].set(jnp.swapaxes(pvec, 1, 2))
        o_cf, o_cb = delta_scan(qc, kc, vc, bac, bar, pc, pr, B, S, L, HC)
        o_c = delta_post(o_cf.reshape(T, MIX_W), o_cb.reshape(T, MIX_W), rest, delta_norm_w[l].reshape(1, DKC), T, MIX_W, z_blk,
                         tm_tok)

        merged = merge_branches(o_a.reshape(T, MIX_W), o_b.reshape(T, MIX_W), o_c, w_branch[l].astype(BF16),
                                rest, T, D, MIX_W, gate_off // tn_mm, tm_mm, tn_mm)
        x1, a2, a2p = outproj_ln(merged, w_out[l].astype(BF16), h, ln1_g[l].reshape(1, D), ln1_b[l].reshape(1, D),
                            mod4[l], B, S, L, alpha, tm_tok)

        w_r = jnp.pad(w_router[l], ((0, 0), (0, LANE - E)))
        b_r = jnp.pad(router_bias[l], (0, LANE - E)).reshape(1, LANE)
        gate, idx = router(a2, w_r, b_r, E, tm_tok)
        plan = route_plan(idx[:, :TOP_K], gate, E, tme)
        f_routed = experts_routed(a2p, plan, w_gate_e[l].astype(BF16), w_up_e[l].astype(BF16),
                                  w_down_e[l].astype(BF16), tme).reshape(TOP_K + 1, T * (D // 256), LANE)
        wgu_s = jnp.concatenate([w_gate_s[l], w_up_s[l]], axis=-1).astype(BF16)
        h, a1 = shared_ln(a2, f_routed, x1, wgu_s, w_down_s[l].astype(BF16), ln2_g[l].reshape(1, D),
                          ln2_b[l].reshape(1, D), mod4[l], mod4[min(l + 1, depth - 1)], B, S, L, alpha, tm_tok)

    return h.reshape(B, S, D)[:, :L]
```

```python
import functools
import math

import jax
import jax.numpy as jnp
from jax import lax
from jax.experimental import pallas as pl
from jax.experimental.pallas import tpu as pltpu

F32 = jnp.float32
BF16 = jnp.bfloat16
HIGHEST = lax.Precision.HIGHEST

LANE = 128
DA = 64
DB = 128
HB_KV = 2
DKC = 128
CONV_K = 5
CHUNK = 64
BLOCK = 128
WINDOW = 128
GRID_W = 64
ROPE_BASE = 10000.0
TOP_K = 8
ROUTED_SCALE = 2.5
LN_EPS = 1e-5
RMS_EPS = 1e-6
NEG = -1e30
VMEM_LIMIT = 56 * 1024 * 1024


def _cparams(sem):
    return pltpu.CompilerParams(dimension_semantics=sem, vmem_limit_bytes=VMEM_LIMIT)


def _dot(a, b, precision=None):
    return jnp.dot(a, b, preferred_element_type=F32, precision=precision)


def _dot_nt(a, b, precision=None):
    return lax.dot_general(a, b, (((1,), (1,)), ((), ())), preferred_element_type=F32, precision=precision)


def _dot_tn(a, b, precision=None):
    return lax.dot_general(a, b, (((0,), (0,)), ((), ())), preferred_element_type=F32, precision=precision)


def _silu(x):
    return x * jax.nn.sigmoid(x)


def _softplus(x):
    return jnp.maximum(x, 0.0) + jnp.log1p(jnp.exp(-jnp.abs(x)))


def _layer_norm(y, g, b):
    mu = jnp.mean(y, axis=-1, keepdims=True)
    yc = y - mu
    var = jnp.mean(yc * yc, axis=-1, keepdims=True)
    return yc * lax.rsqrt(var + LN_EPS) * g + b


def _pick(tile, S, L):
    assert S % tile == 0, (S, tile)
    return tile


def _is_ctx_rows(i, tm, S, L):
    row = (i * tm) % S + lax.broadcasted_iota(jnp.int32, (tm, 1), 0)
    return row >= L


def _mods_kernel(c_ref, w_ref, b_ref, o_ref):
    o_ref[0] = _dot(_silu(c_ref[...]), w_ref[0], HIGHEST) + b_ref[0]


def compute_mods(c_all, w_ada, b_ada):
    depth, D, n6 = w_ada.shape
    mb = c_all.shape[0]
    tn = 1024
    return pl.pallas_call(
        _mods_kernel,
        grid=(depth, n6 // tn),
        in_specs=[pl.BlockSpec((mb, D), lambda l, j: (0, 0)),
                  pl.BlockSpec((1, D, tn), lambda l, j: (l, 0, j)),
                  pl.BlockSpec((1, 1, tn), lambda l, j: (l, 0, j))],
        out_specs=pl.BlockSpec((1, mb, tn), lambda l, j: (l, 0, j)),
        out_shape=jax.ShapeDtypeStruct((depth, mb, n6), F32),
        compiler_params=_cparams(("arbitrary", "arbitrary")),
        name="mods",
    )(c_all, w_ada, b_ada.reshape(depth, 1, n6))


def _mod_specs(parts, tm, S, B, D):
    specs = []
    for p in parts:
        specs.append(pl.BlockSpec((1, 1, 1, D), lambda i, p=p: ((i * tm) // S, p, 0, 0)))
        specs.append(pl.BlockSpec((1, 1, 1, D), lambda i, p=p: (B, p, 0, 0)))
    return specs


def _modulate_kernel(x_ref, shl_ref, shc_ref, scl_ref, scc_ref, a_ref, *, S, L):
    tm = x_ref.shape[0]
    ctx = _is_ctx_rows(pl.program_id(0), tm, S, L)
    sh = jnp.where(ctx, shc_ref[0, 0], shl_ref[0, 0])
    sc = jnp.where(ctx, scc_ref[0, 0], scl_ref[0, 0])
    a_ref[...] = (x_ref[...] * (1.0 + sc) + sh).astype(a_ref.dtype)


def modulate(x, mod4, B, S, L, tm):
    T, D = x.shape
    return pl.pallas_call(
        functools.partial(_modulate_kernel, S=S, L=L),
        grid=(T // tm,),
        in_specs=[pl.BlockSpec((tm, D), lambda i: (i, 0))] + _mod_specs((0, 1), tm, S, B, D),
        out_specs=pl.BlockSpec((tm, D), lambda i: (i, 0)),
        out_shape=jax.ShapeDtypeStruct((T, D), BF16),
        compiler_params=_cparams(("parallel",)),
        name="modulate",
    )(x, mod4, mod4, mod4, mod4)


def _mm_kernel(a_ref, w_ref, *rest, quarter):
    o_ref = rest[-1]
    acc = _dot(a_ref[...], w_ref[...])
    if quarter:
        cos_ref, sin_ref = rest[0], rest[1]
        tn = acc.shape[1]
        lane = lax.broadcasted_iota(jnp.int32, acc.shape, 1)
        first = (lane % (2 * quarter)) < quarter
        rot = jnp.where(first, pltpu.roll(acc, tn - quarter, 1), pltpu.roll(acc, quarter, 1))
        acc = acc * cos_ref[...] + rot * sin_ref[...]
    o_ref[...] = acc.astype(o_ref.dtype)


def token_matmul(a, w, tm, tn, out_dtype, rope=None, S=None):
    T, K = a.shape
    N = w.shape[1]
    assert T % tm == 0 and N % tn == 0
    in_specs = [pl.BlockSpec((tm, K), lambda i, j: (i, 0)),
                pl.BlockSpec((K, tn), lambda i, j: (0, j))]
    args = [a, w]
    quarter = 0
    if rope is not None:
        cos, sin, quarter = rope
        nper = S // tm
        in_specs += [pl.BlockSpec((tm, tn), lambda i, j: (i % nper, 0)),
                     pl.BlockSpec((tm, tn), lambda i, j: (i % nper, 0))]
        args += [cos, sin]
    return pl.pallas_call(
        functools.partial(_mm_kernel, quarter=quarter),
        grid=(T // tm, N // tn),
        in_specs=in_specs,
        out_specs=pl.BlockSpec((tm, tn), lambda i, j: (i, j)),
        out_shape=jax.ShapeDtypeStruct((T, N), out_dtype),
        compiler_params=_cparams(("parallel", "arbitrary")),
        name="token_matmul",
    )(*args)


def rope_tables(L, C, d, tn):
    half, qt = d // 2, d // 4
    rows = L // GRID_W
    row = jnp.repeat(jnp.arange(rows), GRID_W)
    col = jnp.tile(jnp.arange(GRID_W), rows)
    inv = jnp.power(ROPE_BASE, -jnp.arange(0, half, 2, dtype=F32) / half)
    ang_r = row.astype(F32)[:, None] * inv[None, :]
    ang_c = col.astype(F32)[:, None] * inv[None, :]
    ang = jnp.concatenate([ang_r, ang_r, ang_c, ang_c], axis=-1)
    sign = jnp.where((jnp.arange(d) % half) < qt, -1.0, 1.0).astype(F32)
    cos = jnp.concatenate([jnp.cos(ang), jnp.ones((C, d), F32)], axis=0)
    sin = jnp.concatenate([jnp.sin(ang) * sign[None, :], jnp.zeros((C, d), F32)], axis=0)
    return jnp.tile(cos, (1, tn // d)), jnp.tile(sin, (1, tn // d)), qt


def _diff_attn_kernel(lam_ref, q_ref, k_ref, v_ref, nw_ref, o_ref, *, L, nlat):
    qi = pl.program_id(2)
    q = q_ref[0] * (DA ** -0.5)
    lane = lax.broadcasted_iota(jnp.int32, q.shape, 1)
    q1 = jnp.where(lane < DA, q, jnp.zeros_like(q))
    q2 = jnp.where(lane >= DA, q, jnp.zeros_like(q))

    k = k_ref[0]
    v = v_ref[0]
    tq, S = q.shape[0], k.shape[0]

    vx = jnp.concatenate([v, jnp.ones_like(v)], axis=1)

    def attend(valid):
        def one_map(qm):
            s = _dot_nt(qm, k)
            if valid is not None:
                s = jnp.where(valid, s, NEG)
            e = jnp.exp((s - jnp.max(s, axis=-1, keepdims=True)).astype(BF16))
            ol = _dot(e, vx)
            return ol[:, :LANE] / ol[:, LANE:]

        o = one_map(q1) - lam_ref[0] * one_map(q2)
        o = o * lax.rsqrt(jnp.mean(o * o, axis=-1, keepdims=True) + RMS_EPS) * nw_ref[...]
        o_ref[0] = o.astype(o_ref.dtype)

    @pl.when(qi < nlat)
    def _():
        attend(None)

    @pl.when(qi >= nlat)
    def _():
        row = qi * tq + lax.broadcasted_iota(jnp.int32, (tq, 1), 0)
        kidx = lax.broadcasted_iota(jnp.int32, (tq, S), 1)
        attend((kidx >= L) | (row < L))


def diff_attention(qk_a, rest, lam, norm_w_scaled, B, S, L, HA, tq):
    nlat = L // tq
    return pl.pallas_call(
        functools.partial(_diff_attn_kernel, L=L, nlat=nlat),
        grid=(B, HA, S // tq),
        in_specs=[pl.BlockSpec(memory_space=pltpu.SMEM),
                  pl.BlockSpec((1, tq, LANE), lambda b, h, i: (b, i, h)),
                  pl.BlockSpec((1, S, LANE), lambda b, h, i: (b, 0, HA + h)),
                  pl.BlockSpec((1, S, LANE), lambda b, h, i: (b, 0, h)),
                  pl.BlockSpec((1, LANE), lambda b, h, i: (0, 0))],
        out_specs=pl.BlockSpec((1, tq, LANE), lambda b, h, i: (b, i, h)),
        out_shape=jax.ShapeDtypeStruct((B, S, HA * LANE), BF16),
        compiler_params=_cparams(("parallel", "parallel", "arbitrary")),
        name="diff_attention",
    )(lam, qk_a, qk_a, rest, norm_w_scaled)


def _swa_kernel(sink_ref, q_ref, kp_ref, kc_ref, kn_ref, kx_ref, vp_ref, vc_ref, vn_ref, vx_ref, o_ref,
                *, L, nb, G):
    h = pl.program_id(1)
    n = pl.program_id(2)
    q = jnp.concatenate([q_ref[0, :, g * DB:(g + 1) * DB] for g in range(G)], axis=0)
    k_loc = jnp.concatenate([kp_ref[0], kc_ref[0], kn_ref[0]], axis=0)
    v_loc = jnp.concatenate([vp_ref[0], vc_ref[0], vn_ref[0]], axis=0)
    rows = G * BLOCK
    scale = DB ** -0.5
    s_loc = _dot_nt(q, k_loc) * scale
    s_ctx = _dot_nt(q, kx_ref[0]) * scale
    r = lax.broadcasted_iota(jnp.int32, (rows, 3 * BLOCK), 0)
    j = lax.broadcasted_iota(jnp.int32, (rows, 3 * BLOCK), 1)
    qpos = n * BLOCK + r % BLOCK
    kpos = (n - 1) * BLOCK + j
    rel = kpos - qpos
    ok = (rel <= WINDOW) & (rel >= -WINDOW) & (kpos >= 0) & (kpos < L) & (n < nb)
    s_loc = jnp.where(ok, s_loc, NEG)
    g_of_row = lax.broadcasted_iota(jnp.int32, (rows, 1), 0) // BLOCK
    sink = jnp.zeros((rows, 1), F32)
    for g in range(G):
        sink = jnp.where(g_of_row == g, sink_ref[h * G + g], sink)
    m = jnp.maximum(jnp.maximum(jnp.max(s_loc, axis=-1, keepdims=True),
                                jnp.max(s_ctx, axis=-1, keepdims=True)), sink)
    e_loc = jnp.exp(s_loc - m)
    e_ctx = jnp.exp(s_ctx - m)
    den = (jnp.sum(e_loc, axis=-1, keepdims=True) + jnp.sum(e_ctx, axis=-1, keepdims=True)
           + jnp.exp(sink - m))
    inv = 1.0 / den
    o = _dot((e_loc * inv).astype(v_loc.dtype), v_loc) + _dot((e_ctx * inv).astype(v_loc.dtype), vx_ref[0])
    o_ref[0] = jnp.concatenate([o[g * BLOCK:(g + 1) * BLOCK] for g in range(G)], axis=1).astype(o_ref.dtype)


def swa_attention(qk_b, rest, sink, B, S, L, C, HB, v_blk0):
    G = HB // HB_KV
    nb = L // BLOCK
    xblk = L // C
    assert L % C == 0

    def loc(off, cblk):
        return lambda b, h, n: (b, jnp.clip(jnp.where(n < nb, n + off, 0), 0, nb - 1), cblk + h)

    kspec = lambda off: pl.BlockSpec((1, BLOCK, DB), loc(off, HB))
    vspec = lambda off: pl.BlockSpec((1, BLOCK, DB), loc(off, v_blk0))
    return pl.pallas_call(
        functools.partial(_swa_kernel, L=L, nb=nb, G=G),
        grid=(B, HB_KV, S // BLOCK),
        in_specs=[pl.BlockSpec(memory_space=pltpu.SMEM),
                  pl.BlockSpec((1, BLOCK, G * DB), lambda b, h, n: (b, n, h)),
                  kspec(-1), kspec(0), kspec(1),
                  pl.BlockSpec((1, C, DB), lambda b, h, n: (b, xblk, HB + h)),
                  vspec(-1), vspec(0), vspec(1),
                  pl.BlockSpec((1, C, DB), lambda b, h, n: (b, xblk, v_blk0 + h))],
        out_specs=pl.BlockSpec((1, BLOCK, G * DB), lambda b, h, n: (b, n, h)),
        out_shape=jax.ShapeDtypeStruct((B, S, HB * DB), BF16),
        compiler_params=_cparams(("parallel", "parallel", "arbitrary")),
        name="swa_attention",
    )(sink, qk_b, qk_b, qk_b, qk_b, qk_b, rest, rest, rest, rest)


def _delta_prep_kernel(x_ref, w_ref, o_ref, xs_ref, *, L, rc, mode):
    S, tc = x_ref.shape[1], x_ref.shape[2]
    pad = CONV_K // 2
    halo = 8
    xs_ref[0:halo, :] = jnp.zeros((halo, tc), F32)
    xs_ref[halo + S:halo + S + halo, :] = jnp.zeros((halo, tc), F32)
    xs_ref[halo:halo + S, :] = x_ref[0].astype(F32)
    w = w_ref[...]
    for r0 in range(0, S, rc):
        t = r0 + lax.broadcasted_iota(jnp.int32, (rc, 1), 0)
        acc = jnp.zeros((rc, tc), F32)
        for j in range(CONV_K):
            sl = xs_ref[halo + r0 + j - pad:halo + r0 + j - pad + rc, :]
            src = t + (j - pad)
            if r0 + rc == L and j > pad:
                sl = jnp.where(src < L, sl, 0.0)
            if r0 == L and j < pad:
                sl = jnp.where(src >= L, sl, 0.0)
            acc = acc + sl * w[j:j + 1, :]
        y = _silu(acc)
        if mode != "v":
            parts = []
            for h in range(tc // DKC):
                yh = y[:, h * DKC:(h + 1) * DKC]
                yh = yh * lax.rsqrt(jnp.sum(yh * yh, axis=-1, keepdims=True) + RMS_EPS)
                parts.append(yh * (DKC ** -0.5) if mode == "q" else yh)
            y = jnp.concatenate(parts, axis=1) if len(parts) > 1 else parts[0]
        o_ref[0, r0:r0 + rc, :] = y


def delta_prep(rest, conv_w_l, B, S, L, MIX_W, col_blk0, tc, rc):
    outs = []
    nt = MIX_W // tc
    for pi, mode in enumerate(("q", "k", "v")):
        outs.append(pl.pallas_call(
            functools.partial(_delta_prep_kernel, L=L, rc=rc, mode=mode),
            grid=(B, nt),
            in_specs=[pl.BlockSpec((1, S, tc), lambda b, j, pi=pi: (b, 0, col_blk0 + pi * nt + j)),
                      pl.BlockSpec((CONV_K, tc), lambda b, j, pi=pi: (0, pi * nt + j))],
            out_specs=pl.BlockSpec((1, S, tc), lambda b, j: (b, 0, j)),
            out_shape=jax.ShapeDtypeStruct((B, S, MIX_W), F32),
            scratch_shapes=[pltpu.VMEM((S + 16, tc), F32)],
            compiler_params=_cparams(("parallel", "parallel")),
            name="delta_prep_" + mode,
        )(rest, conv_w_l))
    return outs


def _split16(x):
    hi = x.astype(BF16)
    return hi, (x - hi.astype(F32)).astype(BF16)


def _prod2(x, y):
    return _dot(jnp.concatenate(x, axis=1), jnp.concatenate([y[0], y[1], y[0], y[1]], axis=0))


def _unit_tri_inverse(a_list, ii, jj):
    same16 = (ii // 16) == (jj // 16)
    same32 = (ii // 32) == (jj // 32)
    eye = (ii == jj).astype(F32)
    each = lambda f, *ls: [f(*xs) for xs in zip(*ls)]
    d = each(lambda a: _split16(jnp.where(same16, a, 0.0)), a_list)
    l1 = each(lambda a: _split16(jnp.where(same32 & jnp.logical_not(same16), a, 0.0)), a_list)
    l2 = each(lambda a: _split16(jnp.where(same32, 0.0, a)), a_list)
    d2 = each(lambda x: _split16(_prod2(x, x)), d)
    d4 = each(lambda x: _split16(_prod2(x, x)), d2)
    d8 = each(lambda x: _split16(_prod2(x, x)), d4)
    t = each(lambda a: eye - jnp.where(same16, a, 0.0), a_list)
    for p in (d2, d4, d8):
        t = each(lambda tt, pp: tt + _prod2(_split16(tt), pp), t, p)
    for l in (l1, l2):
        ts = each(_split16, t)
        m = each(lambda tt, ll: _split16(_prod2(tt, ll)), ts, l)
        t = each(lambda tf, mm, tt: tf - _prod2(mm, tt), t, m, ts)
    return t


def _delta_scan_kernel(qf_ref, kf_ref, vf_ref, qb_ref, kb_ref, vb_ref, bacf_ref, bacb_ref, barf_ref, barb_ref,
                       pc_ref, pr_ref, of_ref, ob_ref, s_ref, *, HC):
    @pl.when(pl.program_id(1) == 0)
    def _():
        s_ref[...] = jnp.zeros_like(s_ref)

    ii = lax.broadcasted_iota(jnp.int32, (CHUNK, 2 * CHUNK), 0)
    jj = lax.broadcasted_iota(jnp.int32, (CHUNK, 2 * CHUNK), 1) % CHUNK
    dirs = ((qf_ref, kf_ref, vf_ref, bacf_ref, barf_ref, of_ref), (qb_ref, kb_ref, vb_ref, bacb_ref, barb_ref, ob_ref))
    chains = []
    for dn, (q_ref, k_ref, v_ref, bac_ref, bar_ref, o_ref) in enumerate(dirs):
        ahead = (ii - jj) if dn == 0 else (jj - ii)
        strict = ahead > 0
        incl = ahead >= 0
        tri2 = incl.astype(F32)
        tri = tri2[:, :CHUNK]
        bac = bac_ref[0, 0]
        pc = pc_ref[dn]
        beta_c = jax.nn.sigmoid(bac)
        g_c = -jnp.exp(pc[0:1]) * _softplus(bac + pc[1:2])
        gc_c = _dot(tri, g_c, HIGHEST)
        gtot_c = jnp.sum(g_c, axis=0, keepdims=True)
        bar = bar_ref[0, 0, 0]
        pr = pr_ref[dn]
        g_r = -jnp.exp(pr[:, 0:1]) * _softplus(bar + pr[:, 1:2])
        gc_r = _dot_nt(g_r, jnp.concatenate([tri, tri], axis=0), HIGHEST)
        for h in range(HC):
            chains.append(dict(dn=dn, h=h, q_ref=q_ref, k_ref=k_ref, v_ref=v_ref, o_ref=o_ref, strict=strict,
                               incl=incl, beta=beta_c[:, h:h + 1], gcol=gc_c[:, HC + h:HC + h + 1],
                               grow=gc_r[HC + h:HC + h + 1, :], gtot=gtot_c[:, HC + h:HC + h + 1]))

    sl = lambda ch: slice(ch["h"] * DKC, (ch["h"] + 1) * DKC)
    for ch in chains:
        ch["k"] = ch["k_ref"][0, :, sl(ch)]
        ch["k16"] = ch["k"].astype(BF16)
        ch["kb"] = ch["k"] * ch["beta"]
        ch["dec"] = jnp.exp(jnp.where(ch["incl"], ch["gcol"] - ch["grow"], NEG))
        ch["egc"] = jnp.exp(ch["gcol"])
    for ch in chains:
        k2 = jnp.concatenate([ch["k16"], ch["k16"]], axis=0)
        ch["a"] = _dot_nt(ch["kb"].astype(BF16), k2) * jnp.where(ch["strict"], ch["dec"], 0.0)
    t_list = _unit_tri_inverse([ch["a"] for ch in chains], ii, jj)
    for ch, t in zip(chains, t_list):
        v = ch["v_ref"][0, :, sl(ch)]
        rhs = jnp.concatenate([v * ch["beta"], ch["kb"] * ch["egc"]], axis=1).astype(BF16)
        ch["uw"] = _dot(t[:, :CHUNK].astype(BF16), rhs)
    for ch in chains:
        q = ch["q_ref"][0, :, sl(ch)]
        ch["qk"] = (_dot_nt(q.astype(BF16), ch["k16"]) * ch["dec"][:, :CHUNK]).astype(BF16)
        ch["state"] = s_ref[ch["dn"], ch["h"]]
        lhs = jnp.concatenate([ch["uw"][:, DKC:], q * ch["egc"]], axis=0).astype(BF16)
        ch["ws_qs"] = _dot(lhs, ch["state"].astype(BF16))
    for ch in chains:
        ch["u16"] = (ch["uw"][:, :DKC] - ch["ws_qs"][:CHUNK]).astype(BF16)
    for ch in chains:
        ch["o_ref"][0, :, sl(ch)] = ch["ws_qs"][CHUNK:] + _dot(ch["qk"], ch["u16"])
        k_tail = (ch["k"] * jnp.exp(ch["gtot"] - ch["gcol"])).astype(BF16)
        s_ref[ch["dn"], ch["h"]] = ch["state"] * jnp.exp(ch["gtot"]) + _dot_tn(k_tail, ch["u16"])


def delta_scan(qc, kc, vc, bac, bar, pc, pr, B, S, L, HC):
    nch = S // CHUNK
    nl = L // CHUNK
    fwd = lambda b, c: (b, (c + nl) % nch, 0)
    bwd = lambda b, c: (b, nch - 1 - c, 0)
    mw = HC * DKC
    row = lambda m: pl.BlockSpec((1, CHUNK, mw), m)
    bac_spec = lambda dn, m: pl.BlockSpec((1, 1, CHUNK, LANE), lambda b, c: (dn,) + m(b, c))
    bar_spec = lambda dn, m: pl.BlockSpec((1, 1, 1, 2 * HC, CHUNK), lambda b, c: (dn,) + m(b, c) + (0,))
    return pl.pallas_call(
        functools.partial(_delta_scan_kernel, HC=HC),
        grid=(B, nch),
        in_specs=[row(fwd), row(fwd), row(fwd), row(bwd), row(bwd), row(bwd),
                  bac_spec(0, fwd), bac_spec(1, bwd), bar_spec(0, fwd), bar_spec(1, bwd),
                  pl.BlockSpec((2, 8, LANE), lambda b, c: (0, 0, 0)),
                  pl.BlockSpec((2, 2 * HC, LANE), lambda b, c: (0, 0, 0))],
        out_specs=[row(fwd), row(bwd)],
        out_shape=[jax.ShapeDtypeStruct((B, S, mw), F32), jax.ShapeDtypeStruct((B, S, mw), F32)],
        scratch_shapes=[pltpu.VMEM((2, HC, DKC, DKC), F32)],
        compiler_params=_cparams(("parallel", "arbitrary")),
        name="delta_scan",
    )(qc, kc, vc, qc, kc, vc, bac, bac, bar, bar, pc, pr)


def _delta_post_kernel(of_ref, ob_ref, z_ref, nw_ref, y_ref):
    o = of_ref[...] + ob_ref[...]
    z = z_ref[...].astype(F32)
    nw = nw_ref[...]
    parts = []
    for h in range(o.shape[1] // DKC):
        oh = o[:, h * DKC:(h + 1) * DKC]
        oh = oh * lax.rsqrt(jnp.mean(oh * oh, axis=-1, keepdims=True) + RMS_EPS) * nw
        parts.append(oh * _silu(z[:, h * DKC:(h + 1) * DKC]))
    y_ref[...] = jnp.concatenate(parts, axis=1).astype(y_ref.dtype)


def delta_post(o_f, o_b, rest, norm_w, T, MIX_W, z_blk, tm):
    return pl.pallas_call(
        _delta_post_kernel,
        grid=(T // tm,),
        in_specs=[pl.BlockSpec((tm, MIX_W), lambda i: (i, 0)),
                  pl.BlockSpec((tm, MIX_W), lambda i: (i, 0)),
                  pl.BlockSpec((tm, MIX_W), lambda i: (i, z_blk)),
                  pl.BlockSpec((1, DKC), lambda i: (0, 0))],
        out_specs=pl.BlockSpec((tm, MIX_W), lambda i: (i, 0)),
        out_shape=jax.ShapeDtypeStruct((T, MIX_W), BF16),
        compiler_params=_cparams(("parallel",)),
        name="delta_post",
    )(o_f, o_b, rest, norm_w)


def _merge_kernel(oa_ref, ob_ref, oc_ref, w_ref, ga_ref, gb_ref, gc_ref, m_ref):
    acc = jax.nn.sigmoid(ga_ref[...].astype(F32)) * _dot(oa_ref[...], w_ref[0])
    acc = acc + jax.nn.sigmoid(gb_ref[...].astype(F32)) * _dot(ob_ref[...], w_ref[1])
    acc = acc + jax.nn.sigmoid(gc_ref[...].astype(F32)) * _dot(oc_ref[...], w_ref[2])
    m_ref[...] = acc.astype(m_ref.dtype)


def merge_branches(oa, ob, oc, w_branch, rest, T, D, MIX_W, gate_blk0, tm, tn):
    nd = D // tn
    o_spec = pl.BlockSpec((tm, MIX_W), lambda i, j: (i, 0))
    g_spec = lambda br: pl.BlockSpec((tm, tn), lambda i, j, br=br: (i, gate_blk0 + br * nd + j))
    return pl.pallas_call(
        _merge_kernel,
        grid=(T // tm, nd),
        in_specs=[o_spec, o_spec, o_spec,
                  pl.BlockSpec((3, MIX_W, tn), lambda i, j: (0, 0, j)),
                  g_spec(0), g_spec(1), g_spec(2)],
        out_specs=pl.BlockSpec((tm, tn), lambda i, j: (i, j)),
        out_shape=jax.ShapeDtypeStruct((T, D), BF16),
        compiler_params=_cparams(("parallel", "arbitrary")),
        name="merge_branches",
    )(oa, ob, oc, w_branch, rest, rest, rest)


def _store_packed(ref, y):
    n, d = y.shape
    groups = d // (2 * LANE)

    def rne_bits(v):
        b = lax.bitcast_convert_type(v, jnp.uint32)
        return b + jnp.uint32(0x7FFF) + ((b >> 16) & jnp.uint32(1))

    for g in range(groups):
        lo = y[:, 2 * LANE * g:2 * LANE * g + LANE]
        hi = y[:, 2 * LANE * g + LANE:2 * LANE * (g + 1)]
        ref[pl.ds(g, n, stride=groups), :] = (rne_bits(hi) & jnp.uint32(0xFFFF0000)) | (rne_bits(lo) >> 16)


def _load_packed(ref, n, groups, base=()):
    slabs = []
    for g in range(groups):
        w = ref[base + (pl.ds(g, n, stride=groups), slice(None))]
        slabs.append(lax.bitcast_convert_type(w << 16, F32))
        slabs.append(lax.bitcast_convert_type(w & jnp.uint32(0xFFFF0000), F32))
    return slabs


def _outproj_ln_kernel(m_ref, w_ref, x_ref, g_ref, b_ref, gl_ref, gc_ref, shl_ref, shc_ref, scl_ref, scc_ref,
                       x1_ref, a2_ref, a2p_ref, *, S, L, alpha):
    tm = x_ref.shape[0]
    ctx = _is_ctx_rows(pl.program_id(0), tm, S, L)
    gate = jnp.where(ctx, gc_ref[0, 0], gl_ref[0, 0])
    y = alpha * x_ref[...] + gate * _dot(m_ref[...], w_ref[...])
    x1 = _layer_norm(y, g_ref[...], b_ref[...])
    x1_ref[...] = x1
    sh = jnp.where(ctx, shc_ref[0, 0], shl_ref[0, 0])
    sc = jnp.where(ctx, scc_ref[0, 0], scl_ref[0, 0])
    a2 = x1 * (1.0 + sc) + sh
    a2_ref[...] = a2
    _store_packed(a2p_ref, a2)


def outproj_ln(merged, w_out, x, ln_g, ln_b, mod4, B, S, L, alpha, tm):
    T, D = x.shape
    row = pl.BlockSpec((tm, D), lambda i: (i, 0))
    vec = pl.BlockSpec((1, D), lambda i: (0, 0))
    return pl.pallas_call(
        functools.partial(_outproj_ln_kernel, S=S, L=L, alpha=alpha),
        grid=(T // tm,),
        in_specs=[row, pl.BlockSpec((D, D), lambda i: (0, 0)), row, vec, vec] + _mod_specs((2, 3, 4), tm, S, B, D),
        out_specs=[row, row, pl.BlockSpec((tm * (D // 256), LANE), lambda i: (i, 0))],
        out_shape=[jax.ShapeDtypeStruct((T, D), F32), jax.ShapeDtypeStruct((T, D), F32),
                   jax.ShapeDtypeStruct((T * (D // 256), LANE), jnp.uint32)],
        compiler_params=_cparams(("parallel",)),
        name="outproj_ln",
    )(merged, w_out, x, ln_g, ln_b, *([mod4] * 6))


def _router_kernel(a_ref, w_ref, bias_ref, gate_ref, idx_ref, *, E):
    a_hi, a_lo = _split16(a_ref[...])
    w_hi, w_lo = _split16(w_ref[...])
    logits = _dot(a_hi, w_hi) + (_dot(a_hi, w_lo) + _dot(a_lo, w_hi))
    scores = jax.nn.sigmoid(logits)
    lane = lax.broadcasted_iota(jnp.int32, scores.shape, 1)
    work = jnp.where(lane < E, scores + bias_ref[...], NEG)
    chosen = jnp.zeros(scores.shape, jnp.bool_)
    idx = jnp.zeros(scores.shape, jnp.int32)
    for kk in range(TOP_K):
        m = jnp.max(work, axis=-1, keepdims=True)
        first = jnp.min(jnp.where(work == m, lane, LANE), axis=-1, keepdims=True)
        pick = lane == first
        chosen = jnp.logical_or(chosen, pick)
        idx = jnp.where(lane == kk, first, idx)
        work = jnp.where(pick, 2.0 * NEG, work)
    sel = jnp.where(chosen, scores, 0.0)
    gate_ref[...] = sel / jnp.sum(sel, axis=-1, keepdims=True) * ROUTED_SCALE
    idx_ref[...] = idx


def router(a2, w_router_pad, bias_pad, E, tm):
    T, D = a2.shape
    return pl.pallas_call(
        functools.partial(_router_kernel, E=E),
        grid=(T // tm,),
        in_specs=[pl.BlockSpec((tm, D), lambda i: (i, 0)),
                  pl.BlockSpec((D, LANE), lambda i: (0, 0)),
                  pl.BlockSpec((1, LANE), lambda i: (0, 0))],
        out_specs=[pl.BlockSpec((tm, LANE), lambda i: (i, 0)), pl.BlockSpec((tm, LANE), lambda i: (i, 0))],
        out_shape=[jax.ShapeDtypeStruct((T, LANE), F32), jax.ShapeDtypeStruct((T, LANE), jnp.int32)],
        compiler_params=_cparams(("parallel",)),
        name="router",
    )(a2, w_router_pad, bias_pad)


def route_plan(idx8, gate, E, tme):
    T = idx8.shape[0]
    P = T * TOP_K
    assert T >= 3 * tme
    nt = -(-P // tme) + E + 1
    e_flat = idx8.reshape(P)
    sorted_p = jnp.sort(e_flat * P + jnp.arange(P, dtype=jnp.int32)) % P
    experts = jnp.arange(E, dtype=jnp.int32)
    counts = jnp.sum((e_flat[:, None] == experts[None, :]).astype(jnp.int32), axis=0)
    starts = jnp.cumsum(counts) - counts
    tiles_e = (counts + tme - 1) // tme
    tile_cum = jnp.cumsum(tiles_e)
    n_used = tile_cum[-1:].astype(jnp.int32)
    tile_ids = jnp.arange(nt, dtype=jnp.int32)
    te = jnp.minimum(jnp.sum((tile_ids[:, None] >= tile_cum[None, :]).astype(jnp.int32), axis=1), E - 1)
    local = tile_ids - (tile_cum - tiles_e)[te]
    lane = jnp.arange(tme, dtype=jnp.int32)[None, :]
    row_in_e = local[:, None] * tme + lane
    valid = (row_in_e < counts[te][:, None]) & (tile_ids[:, None] < n_used)
    pair = sorted_p[jnp.clip(starts[te][:, None] + row_in_e, 0, P - 1)]
    tok = jnp.where(valid, pair // TOP_K, 0)
    dummy = P + jnp.where(tile_ids < n_used, tile_ids % 2, 2)[:, None] * tme + lane
    dst = jnp.where(valid, (pair % TOP_K) * T + pair // TOP_K, dummy)
    wts = jnp.take_along_axis(gate, idx8, axis=1).reshape(P)
    w = jnp.where(valid, wts[pair], 0.0).reshape(nt * tme, 1)
    return te, n_used, tok.reshape(nt, 1, tme), dst.reshape(nt, 1, tme), w


def _experts_kernel(te_ref, nu_ref, tok0_ref, tok1_ref, tokn_ref, dstp_ref, w_ref, wg_ref, wu_ref, wd_ref, x_hbm,
                    y_hbm, xb0, xb1, xb2, yb0, yb1, yb2, gsem, ssem, *, tme):
    i = pl.program_id(0)
    nu = nu_ref[0]
    xbufs, ybufs = (xb0, xb1, xb2), (yb0, yb1, yb2)
    G = xb0.shape[0] // tme

    def row_in(t, s, r):
        r0 = r * G if isinstance(r, int) else pl.multiple_of(r * G, G)
        return pltpu.make_async_copy(x_hbm.at[pl.ds(pl.multiple_of(t * G, G), G)], xbufs[s].at[pl.ds(r0, G)],
                                     gsem.at[s])

    def row_out(t, s, r):
        return pltpu.make_async_copy(ybufs[s].at[pl.ds(r * G, G)], y_hbm.at[pl.ds(pl.multiple_of(t * G, G), G)],
                                     ssem.at[s])

    def wait_in(s):
        pltpu.make_async_copy(x_hbm.at[pl.ds(0, tme * G)], xbufs[s], gsem.at[s]).wait()

    def wait_out(s):
        pltpu.make_async_copy(ybufs[s], y_hbm.at[pl.ds(0, tme * G)], ssem.at[s]).wait()

    @pl.when(i == 0)
    def _():
        yb2[...] = jnp.zeros_like(yb2)

        def body(r, carry):
            row_in(tok0_ref[0, 0, r], 0, r).start()
            row_in(tok1_ref[0, 0, r], 1, r).start()
            return carry
        lax.fori_loop(0, tme, body, 0, unroll=8)

    def step(s, compute):
        prev, nxt = (s + 2) % 3, (s + 1) % 3
        wait_in(s)

        @pl.when(i >= 2)
        def _():
            wait_out(s)

        if compute:
            x16 = jnp.concatenate(_load_packed(xbufs[s], tme, G), axis=1).astype(BF16)
            hid = (_silu(_dot(x16, wg_ref[0])) * _dot(x16, wu_ref[0]) * w_ref[...]).astype(BF16)
            _store_packed(ybufs[s], _dot(hid, wd_ref[0]))
        for r in range(tme):
            row_out(dstp_ref[0, 0, r], prev, r).start(priority=1)
            if compute:
                row_in(tokn_ref[0, 0, r], prev, r).start()
        if not compute:
            wait_out(prev)
            wait_out(nxt)
            wait_in(nxt)

    for s in range(3):
        @pl.when((i < nu) & (i % 3 == s))
        def _(s=s):
            step(s, True)

        @pl.when((i == nu) & (i % 3 == s))
        def _(s=s):
            step(s, False)


def experts_routed(a2p, plan, wg, wu, wd, tme):
    te, n_used, tok, dst, w = plan
    E, D, de = wg.shape
    G = D // (2 * LANE)
    T = a2p.shape[0] // G
    nt = tok.shape[0]
    P = T * TOP_K
    smem_tile = lambda m: pl.BlockSpec((1, 1, tme), m, memory_space=pltpu.SMEM)
    grid_spec = pltpu.PrefetchScalarGridSpec(
        num_scalar_prefetch=2,
        grid=(nt,),
        in_specs=[smem_tile(lambda i, te, nu: (0, 0, 0)),
                  smem_tile(lambda i, te, nu: (1, 0, 0)),
                  smem_tile(lambda i, te, nu: (jnp.minimum(i + 2, nt - 1), 0, 0)),
                  smem_tile(lambda i, te, nu: (jnp.where(i == 0, nt - 1, i - 1), 0, 0)),
                  pl.BlockSpec((tme, 1), lambda i, te, nu: (i, 0)),
                  pl.BlockSpec((1, D, de), lambda i, te, nu: (te[i], 0, 0)),
                  pl.BlockSpec((1, D, de), lambda i, te, nu: (te[i], 0, 0)),
                  pl.BlockSpec((1, de, D), lambda i, te, nu: (te[i], 0, 0)),
                  pl.BlockSpec(memory_space=pl.ANY)],
        out_specs=pl.BlockSpec(memory_space=pl.ANY),
        scratch_shapes=[pltpu.VMEM((tme * G, LANE), jnp.uint32)] * 6
        + [pltpu.SemaphoreType.DMA((3,)), pltpu.SemaphoreType.DMA((3,))])
    return pl.pallas_call(
        functools.partial(_experts_kernel, tme=tme),
        grid_spec=grid_spec,
        out_shape=jax.ShapeDtypeStruct(((P + T) * G, LANE), jnp.uint32),
        compiler_params=_cparams(("arbitrary",)),
        name="experts_routed",
    )(te, n_used, tok, tok, tok, dst, w, wg, wu, wd, a2p)


def _shared_ln_kernel(a_ref, f_ref, x_ref, wgu_ref, wd_ref, g_ref, b_ref, gl_ref, gc_ref, shl_ref, shc_ref,
                      scl_ref, scc_ref, x2_ref, a1_ref, *, S, L, alpha, DS):
    tm = x_ref.shape[0]
    ctx = _is_ctx_rows(pl.program_id(0), tm, S, L)
    gu = _dot(a_ref[...].astype(BF16), wgu_ref[...])
    f = _dot((_silu(gu[:, :DS]) * gu[:, DS:]).astype(BF16), wd_ref[...])
    groups = f_ref.shape[1] // tm
    slabs = _load_packed(f_ref, tm, groups, base=(0,))
    for k in range(1, TOP_K):
        slabs = [a + b for a, b in zip(slabs, _load_packed(f_ref, tm, groups, base=(k,)))]
    f = f + jnp.concatenate(slabs, axis=1)
    gate = jnp.where(ctx, gc_ref[0, 0], gl_ref[0, 0])
    x2 = _layer_norm(alpha * x_ref[...] + gate * f, g_ref[...], b_ref[...])
    x2_ref[...] = x2
    sh = jnp.where(ctx, shc_ref[0, 0], shl_ref[0, 0])
    sc = jnp.where(ctx, scc_ref[0, 0], scl_ref[0, 0])
    a1_ref[...] = (x2 * (1.0 + sc) + sh).astype(a1_ref.dtype)


def shared_ln(a2, f_routed, x1, wgu_s, wd_s, ln_g, ln_b, mod4, mod4_next, B, S, L, alpha, tm):
    T, D = x1.shape
    ds2 = wgu_s.shape[1]
    row = pl.BlockSpec((tm, D), lambda i: (i, 0))
    vec = pl.BlockSpec((1, D), lambda i: (0, 0))
    return pl.pallas_call(
        functools.partial(_shared_ln_kernel, S=S, L=L, alpha=alpha, DS=ds2 // 2),
        grid=(T // tm,),
        in_specs=[row, pl.BlockSpec((TOP_K, tm * (D // 256), LANE), lambda i: (0, i, 0)), row,
                  pl.BlockSpec((D, ds2), lambda i: (0, 0)),
                  pl.BlockSpec((ds2 // 2, D), lambda i: (0, 0)),
                  vec, vec] + _mod_specs((5,), tm, S, B, D) + _mod_specs((0, 1), tm, S, B, D),
        out_specs=[row, row],
        out_shape=[jax.ShapeDtypeStruct((T, D), F32), jax.ShapeDtypeStruct((T, D), BF16)],
        compiler_params=_cparams(("parallel",)),
        name="shared_ln",
    )(a2, f_routed, x1, wgu_s, wd_s, ln_g, ln_b, mod4, mod4, *([mod4_next] * 4))


def _col_layout(MIX_W, D, HC):
    kv = (MIX_W, MIX_W, HB_KV * DB, HB_KV * DB, MIX_W, MIX_W, MIX_W, 2 * HC, 2 * HC)
    qs = (MIX_W, MIX_W, MIX_W, 3 * D)
    names = ("ak", "av", "bk", "bv", "cq", "ck", "cv", "cb", "ca", "qa", "qb", "z", "gates")
    offs, o = {}, 0
    for n, s in zip(names, kv + qs):
        offs[n] = (o, o + s)
        o += s
    return offs


def kernel(x, c, ctx, c_ctx, w_ada, b_ada, w_in, lam_q1, lam_k1, lam_q2, lam_k2, diff_norm_w, sink, conv_w, a_log, dt_bias, delta_norm_w, w_branch, w_out, ln1_g, ln1_b, ln2_g, ln2_b, w_router, router_bias, w_gate_e, w_up_e, w_down_e, w_gate_s, w_up_s, w_down_s):
    B, L, D = x.shape
    C = ctx.shape[1]
    S = L + C
    T = B * S
    depth = w_ada.shape[0]
    MIX_W = D // 2
    HA = MIX_W // (2 * DA)
    HB = MIX_W // DB
    HC = MIX_W // DKC
    E = w_router.shape[-1]
    alpha = (2 * depth) ** 0.25
    assert 2 * HC <= LANE and E <= LANE

    tm_tok = math.gcd(256, math.gcd(L, C))
    tm_mm = S // 3 if S % 3 == 0 and (S // 3) % 16 == 0 else tm_tok
    tn_mm = min(512, MIX_W)
    tq = tm_mm
    if S % 2 == 0 and (S // 2) % 16 == 0:
        tm_mm = S // 2
    tc = min(512, MIX_W)
    rc = math.gcd(256, math.gcd(L, C))
    tme = 256

    mb = -(-(B + 1) // 8) * 8
    c_all = jnp.zeros((mb, D), F32).at[:B].set(c).at[B].set(c_ctx)
    mods = compute_mods(c_all, w_ada, b_ada)
    mod4 = [mods[l].reshape(mb, 6, 1, D) for l in range(depth)]

    cos_a, sin_a, qt_a = rope_tables(L, C, DA, tn_mm)
    cos_b, sin_b, qt_b = rope_tables(L, C, DB, tn_mm if (HB + HB_KV) * DB % tn_mm == 0 else DB * HB_KV)
    tn_b = cos_b.shape[1]

    offs = _col_layout(MIX_W, D, HC)
    col = lambda w, n: w[:, offs[n][0]:offs[n][1]]
    n_rest = 5 * MIX_W + 3 * D + HB_KV * DB
    n_rest_pad = -(-n_rest // tn_mm) * tn_mm
    z_blk = 4
    gate_off = 5 * MIX_W
    bv_off = 5 * MIX_W + 3 * D

    h = jnp.concatenate([x, ctx], axis=1).reshape(T, D)
    a1 = modulate(h, mod4[0], B, S, L, tm_tok)

    for l in range(depth):
        wl = w_in[l].astype(BF16)
        w_qk_a = jnp.concatenate([col(wl, "qa"), col(wl, "ak")], axis=1)
        w_qk_b = jnp.concatenate([col(wl, "qb"), col(wl, "bk")], axis=1)
        w_rest = jnp.concatenate([col(wl, n) for n in ("av", "cq", "ck", "cv", "z", "gates", "bv")]
                                 + [jnp.zeros((D, n_rest_pad - n_rest), BF16)], axis=1)
        w_ba = jnp.concatenate([col(wl, "cb"), col(wl, "ca"), jnp.zeros((D, LANE - 4 * HC), BF16)], axis=1)

        qk_a = token_matmul(a1, w_qk_a, tm_mm, tn_mm, BF16, rope=(cos_a, sin_a, qt_a), S=S)
        qk_b = token_matmul(a1, w_qk_b, tm_mm, tn_b, BF16, rope=(cos_b, sin_b, qt_b), S=S)
        rest = token_matmul(a1, w_rest, tm_mm, tn_mm, BF16)
        ba = token_matmul(a1, w_ba, tm_mm, LANE, F32)

        lam_init = 0.8 - 0.6 * math.exp(-0.3 * l)
        lam = (jnp.exp(jnp.sum(lam_q1[l] * lam_k1[l])) - jnp.exp(jnp.sum(lam_q2[l] * lam_k2[l]))
               + lam_init).reshape(1).astype(F32)
        nw_a = (diff_norm_w[l] * (1.0 - lam_init)).reshape(1, 2 * DA)
        o_a = diff_attention(qk_a.reshape(B, S, -1), rest.reshape(B, S, -1), lam, nw_a, B, S, L, HA, tq)

        o_b = swa_attention(qk_b.reshape(B, S, -1), rest.reshape(B, S, -1), sink[l], B, S, L, C, HB,
                            bv_off // DB)

        qc, kc, vc = delta_prep(rest.reshape(B, S, -1), conv_w[l], B, S, L, MIX_W, MIX_W // tc, tc, rc)
        ba3 = ba.reshape(B, S, LANE)
        cb = ba3[..., :2 * HC].reshape(B, S, 2, HC)
        ca = ba3[..., 2 * HC:4 * HC].reshape(B, S, 2, HC)
        bdir = jnp.moveaxis(jnp.concatenate([cb, ca], axis=-1), 2, 0)
        bac = jnp.pad(bdir, ((0, 0), (0, 0), (0, 0), (0, LANE - 2 * HC)))
        bar = jnp.swapaxes(bdir.reshape(2, B, S // CHUNK, CHUNK, 2 * HC), -1, -2)
        pvec = jnp.stack([a_log[l], dt_bias[l]], axis=1).astype(F32)
        pc = jnp.zeros((2, 8, LANE), F32).at[:, :2, HC:2 * HC].set(pvec)
        pr = jnp.zeros((2, 2 * HC, LANE), F32).at[:, HC:, :2].set(jnp.swapaxes(pvec, 1, 2))
        o_cf, o_cb = delta_scan(qc, kc, vc, bac, bar, pc, pr, B, S, L, HC)
        o_c = delta_post(o_cf.reshape(T, MIX_W), o_cb.reshape(T, MIX_W), rest, delta_norm_w[l].reshape(1, DKC), T, MIX_W, z_blk,
                         tm_tok)

        merged = merge_branches(o_a.reshape(T, MIX_W), o_b.reshape(T, MIX_W), o_c, w_branch[l].astype(BF16),
                                rest, T, D, MIX_W, gate_off // tn_mm, tm_mm, tn_mm)
        x1, a2, a2p = outproj_ln(merged, w_out[l].astype(BF16), h, ln1_g[l].reshape(1, D), ln1_b[l].reshape(1, D),
                            mod4[l], B, S, L, alpha, tm_tok)

        w_r = jnp.pad(w_router[l], ((0, 0), (0, LANE - E)))
        b_r = jnp.pad(router_bias[l], (0, LANE - E)).reshape(1, LANE)
        gate, idx = router(a2, w_r, b_r, E, tm_tok)
        plan = route_plan(idx[:, :TOP_K], gate, E, tme)
        f_routed = experts_routed(a2p, plan, w_gate_e[l].astype(BF16), w_up_e[l].astype(BF16),
                                  w_down_e[l].astype(BF16), tme).reshape(TOP_K + 1, T * (D // 256), LANE)
        wgu_s = jnp.concatenate([w_gate_s[l], w_up_s[l]], axis=-1).astype(BF16)
        h, a1 = shared_ln(a2, f_routed, x1, wgu_s, w_down_s[l].astype(BF16), ln2_g[l].reshape(1, D),
                          ln2_b[l].reshape(1, D), mod4[l], mod4[min(l + 1, depth - 1)], B, S, L, alpha, tm_tok)

    return h.reshape(B, S, D)[:, :L]
```
